```python
import math
import jax, jax.numpy as jnp
from jax import lax
import numpy as np

D_MODEL = 1024
BATCH = 8
SEQ = 2048
DEPTH = 4
DEC_BATCH = 128
DEC_SEQ = 4
PAST_LEN = 16384
PAGE_SIZE = 128

D_A = D_MODEL // 2
D_B = D_MODEL // 2
HEAD_B = 64
H_B = D_B // HEAD_B
LORA_W = 64
LORA_A = 64
LORA_G = 128
D_BP = 3 * D_B + LORA_W + LORA_A + LORA_G
D_IN_AB = 2 * D_A + D_BP
SPLIT_B = (D_B, 2 * D_B, 3 * D_B, 3 * D_B + LORA_W, 3 * D_B + LORA_W + LORA_A)
CONV_A_WIDTH = 31
D_C = D_MODEL
C_GROUP = 16
G_C = D_C // C_GROUP
P_C = 64
D_FF = ((8 * D_MODEL // 3 + 127) // 128) * 128
FFN_CONV_WIDTH = 3
N_AB = (DEPTH + 1) // 2
N_C = DEPTH // 2
ALPHA = (2 * DEPTH) ** 0.25
BETA = (8 * DEPTH) ** -0.25
LN_EPS = 1e-5
GN_EPS = HEAD_B * 1e-5

kernel_name = 'hybrid_conformer_rwkv7_s5_convffn_step'


def layer_norm(x, g, b, eps=LN_EPS):
    xf = x.astype(jnp.float32)
    xc = xf - jnp.mean(xf, -1, keepdims=True)
    var = jnp.mean(xc * xc, -1, keepdims=True)
    return (xc * lax.rsqrt(var + eps) * g.astype(jnp.float32) + b.astype(jnp.float32)).astype(x.dtype)


def causal_depthwise(ext, w, b):
    y = lax.conv_general_dilated(ext, w[:, None, :].astype(ext.dtype), (1,), 'VALID',
                                 dimension_numbers=('NWC', 'WIO', 'NWC'),
                                 feature_group_count=ext.shape[-1])
    return y + b.astype(ext.dtype)


def wkv_recurrence(r, k, v, decay, kk, a, s0):
    def step(s, inp):
        r_t, k_t, v_t, d_t, kk_t, a_t = inp
        sa = jnp.einsum('bhvk,bhk->bhv', s, -kk_t)
        s = (s * d_t[:, :, None, :] + sa[..., None] * (kk_t * a_t)[:, :, None, :]
             + v_t[..., None] * k_t[:, :, None, :])
        return s, jnp.einsum('bhvk,bhk->bhv', s, r_t)
    xs = tuple(jnp.swapaxes(t, 0, 1) for t in (r, k, v, decay, kk, a))
    s, y = lax.scan(step, s0.astype(jnp.float32), xs)
    return jnp.swapaxes(y, 0, 1), s


def mixer_ab(x, conv_st, shift_st, wkv_st, w_in, conv_w, conv_b, ln_g, ln_b, mu, w0, w2, a0, a2, g2,
             k_k, k_a, r_k, lnx_g, lnx_b, w_out):
    bsz, t_len, _ = x.shape
    f32 = jnp.float32
    p = x @ w_in
    pa, pb = p[..., :2 * D_A], p[..., 2 * D_A:]
    u = pa[..., :D_A] * jax.nn.sigmoid(pa[..., D_A:])
    ext = jnp.concatenate([conv_st.astype(u.dtype), u], axis=1)
    ya = jax.nn.silu(layer_norm(causal_depthwise(ext, conv_w, conv_b), ln_g, ln_b)).astype(f32)
    new_conv = ext[:, -(CONV_A_WIDTH - 1):]
    prev = jnp.concatenate([shift_st[:, None].astype(pb.dtype), pb[:, :-1]], axis=1)
    q = (pb + (prev - pb) * mu).astype(f32)
    new_shift = pb[:, -1]
    r, k, v, wl, al, gl = jnp.split(q, SPLIT_B, axis=-1)
    w = -jax.nn.softplus(-(w0 + jnp.tanh(wl) @ w2)) - 0.5
    a = jax.nn.sigmoid(a0 + al @ a2)
    g = jax.nn.sigmoid(gl) @ g2
    heads = lambda t: t.reshape(bsz, t_len, H_B, HEAD_B)
    kk = heads(k * k_k)
    kk = kk * lax.rsqrt(jnp.maximum(jnp.sum(kk * kk, -1, keepdims=True), 1e-24))
    k = k * (1.0 + (a - 1.0) * k_a)
    decay = jnp.exp(-jnp.exp(w))
    r, k, v, decay, a = (heads(t) for t in (r, k, v, decay, a))
    y, new_wkv = wkv_recurrence(r, k, v, decay, kk, a, wkv_st)
    yc = y - jnp.mean(y, -1, keepdims=True)
    y = yc * lax.rsqrt(jnp.mean(yc * yc, -1, keepdims=True) + GN_EPS)
    y = y.reshape(bsz, t_len, D_B) * lnx_g + lnx_b
    y = y + (jnp.sum(r * k * r_k, -1, keepdims=True) * v).reshape(bsz, t_len, D_B)
    yb = y * g
    out = jnp.concatenate([ya, yb], axis=-1) @ w_out
    return out.astype(x.dtype), new_conv, new_shift, new_wkv


def mixer_c(x, h_re, h_im, w_in, lam_re, lam_im, log_dt, b_re, b_im, c_re, c_im, d_skip, w_out):
    bsz, t_len, _ = x.shape
    f32 = jnp.float32
    u = (x @ w_in).astype(f32)
    ug = u.reshape(bsz, t_len, G_C, C_GROUP)
    lr, li = lam_re.astype(f32), lam_im.astype(f32)
    dt = jnp.exp(log_dt.astype(f32))[:, None]
    mag = jnp.exp(lr * dt)
    ab_re, ab_im = mag * jnp.cos(li * dt), mag * jnp.sin(li * dt)
    den = lr * lr + li * li
    nr, ni = ab_re - 1.0, ab_im
    f_re, f_im = (nr * lr + ni * li) / den, (ni * lr - nr * li) / den
    br, bi = b_re.astype(f32), b_im.astype(f32)
    bb_re = f_re[..., None] * br - f_im[..., None] * bi
    bb_im = f_re[..., None] * bi + f_im[..., None] * br
    bu_re = jnp.einsum('gpc,btgc->tbgp', bb_re, ug)
    bu_im = jnp.einsum('gpc,btgc->tbgp', bb_im, ug)
    h_re, h_im = h_re.astype(f32), h_im.astype(f32)
    bu_re = bu_re.at[0].add(ab_re * h_re - ab_im * h_im)
    bu_im = bu_im.at[0].add(ab_re * h_im + ab_im * h_re)
    a_re = jnp.broadcast_to(ab_re, (t_len,) + ab_re.shape)
    a_im = jnp.broadcast_to(ab_im, (t_len,) + ab_im.shape)

    def combine(e1, e2):
        a1r, a1i, b1r, b1i = e1
        a2r, a2i, b2r, b2i = e2
        ar = a2r * a1r - a2i * a1i
        ai = a2r * a1i + a2i * a1r
        xr = a2r[:, None] * b1r - a2i[:, None] * b1i + b2r
        xi = a2r[:, None] * b1i + a2i[:, None] * b1r + b2i
        return ar, ai, xr, xi

    _, _, xs_re, xs_im = lax.associative_scan(combine, (a_re, a_im, bu_re, bu_im), axis=0)
    y = (jnp.einsum('gcp,tbgp->btgc', c_re.astype(f32), xs_re)
         - jnp.einsum('gcp,tbgp->btgc', c_im.astype(f32), xs_im))
    y = y.reshape(bsz, t_len, D_C) + d_skip * u
    o = jax.nn.gelu(y) @ w_out
    out = o[..., :D_MODEL] * jax.nn.sigmoid(o[..., D_MODEL:])
    return out.astype(x.dtype), xs_re[-1], xs_im[-1]


def conv_ffn(x, st, w_up, conv_w, conv_b, w_down):
    hg = x @ w_up
    h, g = hg[..., :D_FF], hg[..., D_FF:]
    ext = jnp.concatenate([st.astype(h.dtype), h], axis=1)
    c = causal_depthwise(ext, conv_w, conv_b)
    y = (jax.nn.silu(c) * g) @ w_down
    return y.astype(x.dtype), ext[:, -(FFN_CONV_WIDTH - 1):]


def trunk(x, st_conv_a, st_shift, st_wkv, st_re, st_im, st_ffn, ab_params, c_params, ffn_params, norms):
    ln1_g, ln1_b, ln2_g, ln2_b = norms
    n_conv, n_shift, n_wkv, n_re, n_im, n_ffn = [], [], [], [], [], []
    for i in range(DEPTH):
        j = i // 2
        if i % 2 == 0:
            out, nc, ns, nw = mixer_ab(x, st_conv_a[:, j], st_shift[:, j], st_wkv[:, j],
                                       *(p[j] for p in ab_params))
            n_conv.append(nc); n_shift.append(ns); n_wkv.append(nw)
        else:
            out, nr, ni = mixer_c(x, st_re[:, j], st_im[:, j], *(p[j] for p in c_params))
            n_re.append(nr); n_im.append(ni)
        x = layer_norm(ALPHA * x + out, ln1_g[i], ln1_b[i])
        f, nf = conv_ffn(x, st_ffn[:, i], *(p[i] for p in ffn_params))
        n_ffn.append(nf)
        x = layer_norm(ALPHA * x + f, ln2_g[i], ln2_b[i])
    return (x, jnp.stack(n_conv, 1), jnp.stack(n_shift, 1), jnp.stack(n_wkv, 1),
            jnp.stack(n_re, 1), jnp.stack(n_im, 1), jnp.stack(n_ffn, 1))


def setup_inputs(seed: int = 0) -> dict:
    key = jax.random.key(seed)
    keys = iter(jax.random.split(key, 64))
    nrm = lambda shape, scale: scale * jax.random.normal(next(keys), shape, jnp.float32)
    uni = lambda shape, lo, hi: jax.random.uniform(next(keys), shape, jnp.float32, lo, hi)
    d = D_MODEL
    return {
        'x_prompt': nrm((BATCH, SEQ, d), 1.0),
        'x_sample': nrm((DEC_BATCH, DEC_SEQ, d), 1.0),
        'state_conv_a': nrm((DEC_BATCH, N_AB, CONV_A_WIDTH - 1, D_A), 0.5),
        'state_shift_b': nrm((DEC_BATCH, N_AB, D_BP), 1.0),
        'state_wkv_b': nrm((DEC_BATCH, N_AB, H_B, HEAD_B, HEAD_B), 0.3),
        'state_ssm_re': nrm((DEC_BATCH, N_C, G_C, P_C), 0.3),
        'state_ssm_im': nrm((DEC_BATCH, N_C, G_C, P_C), 0.3),
        'state_conv_ffn': nrm((DEC_BATCH, DEPTH, FFN_CONV_WIDTH - 1, D_FF), 1.0),
        'w_in_ab': nrm((N_AB, d, D_IN_AB), d ** -0.5),
        'conv_a_w': nrm((N_AB, CONV_A_WIDTH, D_A), CONV_A_WIDTH ** -0.5),
        'conv_a_b': nrm((N_AB, D_A), 0.02),
        'ln_a_g': 1.0 + nrm((N_AB, D_A), 0.02),
        'ln_a_b': nrm((N_AB, D_A), 0.02),
        'mu_b': uni((N_AB, D_BP), 0.0, 1.0),
        'w0_b': uni((N_AB, D_B), -6.0, 1.0),
        'w2_b': nrm((N_AB, LORA_W, D_B), 0.5 * LORA_W ** -0.5),
        'a0_b': nrm((N_AB, D_B), 0.1),
        'a2_b': nrm((N_AB, LORA_A, D_B), LORA_A ** -0.5),
        'g2_b': nrm((N_AB, LORA_G, D_B), LORA_G ** -0.5),
        'k_k_b': 0.85 + nrm((N_AB, D_B), 0.02),
        'k_a_b': 1.0 + nrm((N_AB, D_B), 0.02),
        'r_k_b': nrm((N_AB, H_B, HEAD_B), 0.1),
        'ln_x_g': 1.0 + nrm((N_AB, D_B), 0.02),
        'ln_x_b': nrm((N_AB, D_B), 0.02),
        'w_out_ab': nrm((N_AB, D_A + D_B, d), BETA * (D_A + D_B) ** -0.5),
        'w_in_c': nrm((N_C, d, D_C), d ** -0.5),
        'lam_re': -0.5 + nrm((N_C, G_C, P_C), 0.01),
        'lam_im': math.pi * jnp.arange(P_C, dtype=jnp.float32) + nrm((N_C, G_C, P_C), 0.01),
        'log_dt': uni((N_C, G_C), math.log(1e-3), math.log(1e-1)),
        'b_re': nrm((N_C, G_C, P_C, C_GROUP), (2 * C_GROUP) ** -0.5),
        'b_im': nrm((N_C, G_C, P_C, C_GROUP), (2 * C_GROUP) ** -0.5),
        'c_re': nrm((N_C, G_C, C_GROUP, P_C), P_C ** -0.5),
        'c_im': nrm((N_C, G_C, C_GROUP, P_C), P_C ** -0.5),
        'd_skip': nrm((N_C, D_C), 1.0),
        'w_out_c': nrm((N_C, D_C, 2 * d), BETA * D_C ** -0.5),
        'w_up': nrm((DEPTH, d, 2 * D_FF), d ** -0.5),
        'conv_f_w': nrm((DEPTH, FFN_CONV_WIDTH, D_FF), FFN_CONV_WIDTH ** -0.5),
        'conv_f_b': nrm((DEPTH, D_FF), 0.02),
        'w_down': nrm((DEPTH, D_FF, d), BETA * D_FF ** -0.5),
        'ln1_g': 1.0 + nrm((DEPTH, d), 0.02),
        'ln1_b': nrm((DEPTH, d), 0.02),
        'ln2_g': 1.0 + nrm((DEPTH, d), 0.02),
        'ln2_b': nrm((DEPTH, d), 0.02),
    }


def reference(x_prompt, x_sample, state_conv_a, state_shift_b, state_wkv_b, state_ssm_re, state_ssm_im,
              state_conv_ffn, w_in_ab, conv_a_w, conv_a_b, ln_a_g, ln_a_b, mu_b, w0_b, w2_b, a0_b, a2_b,
              g2_b, k_k_b, k_a_b, r_k_b, ln_x_g, ln_x_b, w_out_ab, w_in_c, lam_re, lam_im, log_dt,
              b_re, b_im, c_re, c_im, d_skip, w_out_c, w_up, conv_f_w, conv_f_b, w_down,
              ln1_g, ln1_b, ln2_g, ln2_b):
    ab_params = (w_in_ab, conv_a_w, conv_a_b, ln_a_g, ln_a_b, mu_b, w0_b, w2_b, a0_b, a2_b, g2_b,
                 k_k_b, k_a_b, r_k_b, ln_x_g, ln_x_b, w_out_ab)
    c_params = (w_in_c, lam_re, lam_im, log_dt, b_re, b_im, c_re, c_im, d_skip, w_out_c)
    ffn_params = (w_up, conv_f_w, conv_f_b, w_down)
    norms = (ln1_g, ln1_b, ln2_g, ln2_b)
    bp = x_prompt.shape[0]
    zeros_like_state = lambda s: jnp.zeros((bp,) + s.shape[1:], s.dtype)
    (y_prompt, conv_a_p, shift_b_p, wkv_b_p, ssm_re_p, ssm_im_p, conv_ffn_p) = trunk(
        x_prompt, zeros_like_state(state_conv_a), zeros_like_state(state_shift_b),
        zeros_like_state(state_wkv_b), zeros_like_state(state_ssm_re), zeros_like_state(state_ssm_im),
        zeros_like_state(state_conv_ffn), ab_params, c_params, ffn_params, norms)
    (y_sample, conv_a_s, shift_b_s, wkv_b_s, ssm_re_s, ssm_im_s, conv_ffn_s) = trunk(
        x_sample, state_conv_a, state_shift_b, state_wkv_b, state_ssm_re, state_ssm_im, state_conv_ffn,
        ab_params, c_params, ffn_params, norms)
    return (y_prompt, y_sample, conv_a_p, conv_a_s, shift_b_p, shift_b_s, wkv_b_p, wkv_b_s,
            ssm_re_p, ssm_re_s, ssm_im_p, ssm_im_s, conv_ffn_p, conv_ffn_s)
```

```python
import functools
import math

import jax
import jax.numpy as jnp
from jax import lax
from jax.experimental import pallas as pl
from jax.experimental.pallas import tpu as pltpu

F32 = jnp.float32
BF16 = jnp.bfloat16

D_MODEL = 1024
DEPTH = 4
D_A = 512
D_B = 512
HEAD_B = 64
H_B = D_B // HEAD_B
LORA_W = 64
LORA_A = 64
LORA_G = 128
D_BP = 3 * D_B + LORA_W + LORA_A + LORA_G
D_IN_AB = 2 * D_A + D_BP
CONV_A_WIDTH = 31
C_GROUP = 16
G_C = D_MODEL // C_GROUP
P_C = 64
D_SSM = G_C * P_C
SLAB = 128
N_SLAB = D_MODEL // SLAB
G_SLAB = SLAB // C_GROUP
SSM_SLAB = G_SLAB * P_C
D_FF = 2816
FFN_CONV_WIDTH = 3
ALPHA = (2 * DEPTH) ** 0.25
LN_EPS = 1e-5
GN_EPS = HEAD_B * 1e-5

VMEM_LIMIT_BYTES = 56 * 1024 * 1024
FFN_CHUNK = 256
WKV_CHUNK = 64


def _params(n_grid):
    return pltpu.CompilerParams(dimension_semantics=("arbitrary",) * n_grid,
                                vmem_limit_bytes=VMEM_LIMIT_BYTES)


def _full(shape):
    nd = len(shape)
    return pl.BlockSpec(shape, lambda *_: (0,) * nd)


def _rows(tm, width):
    return pl.BlockSpec((tm, width), lambda i: (i, 0))


def _dot(a, b):
    return jnp.dot(a.astype(BF16), b.astype(BF16), preferred_element_type=F32)


def _split(a):
    hi = a.astype(BF16)
    lo = (a - hi.astype(F32)).astype(BF16)
    return hi, lo


def _dg(a, b, ca, cb):
    return lax.dot_general(a, b, (((ca,), (cb,)), ((), ())), preferred_element_type=F32)


def _dot_hp(a, b, ca=1, cb=0):
    ah, al = _split(a)
    bh, bl = _split(b)
    return _dg(ah, bh, ca, cb) + (_dg(ah, bl, ca, cb) + _dg(al, bh, ca, cb))


def _layer_norm(x, g, b, eps):
    xc = x - jnp.mean(x, -1, keepdims=True)
    var = jnp.mean(xc * xc, -1, keepdims=True)
    return xc * lax.rsqrt(var + eps) * g + b


def _ab_in_kernel(x_ref, w_ref, cw_ref, cb_ref, lng_ref, lnb_ref, mu_ref, w0_ref, w2_ref, a0_ref,
                  a2_ref, g2_ref, kkw_ref, kaw_ref, seg_ref, conv0_ref, shift0_ref,
                  ya_ref, r_ref, k_ref, v_ref, ld_ref, kk_ref, b_ref, g_ref, convo_ref, shifto_ref,
                  ext_scr, sh_scr, *, bq, tm):
    hist = (CONV_A_WIDTH - 1) * bq

    @pl.when(pl.program_id(0) == 0)
    def _():
        ext_scr[0:hist, :] = conv0_ref[...]
        sh_scr[0:bq, :] = shift0_ref[...]

    p = _dot(x_ref[...], w_ref[...])

    ext_scr[hist:hist + tm, :] = p[:, :D_A] * jax.nn.sigmoid(p[:, D_A:2 * D_A])
    acc = jnp.zeros((tm, D_A), F32) + cb_ref[...]
    for j in range(CONV_A_WIDTH):
        acc = acc + cw_ref[j:j + 1, :] * ext_scr[j * bq:j * bq + tm, :]
    ya = _layer_norm(acc, lng_ref[...], lnb_ref[...], LN_EPS)
    ya_ref[...] = ya * jax.nn.sigmoid(ya)
    new_hist = ext_scr[tm:tm + hist, :]
    convo_ref[...] = new_hist
    ext_scr[0:hist, :] = new_hist

    pb = p[:, 2 * D_A:]
    sh_scr[bq:bq + tm, :] = pb
    prev = sh_scr[0:tm, :]
    last = pb[tm - bq:, :]
    sh_scr[0:bq, :] = last
    shifto_ref[...] = last
    q = pb + (prev - pb) * mu_ref[...]
    r = q[:, :D_B]
    k = q[:, D_B:2 * D_B]
    v = q[:, 2 * D_B:3 * D_B]
    wa = q[:, 3 * D_B:3 * D_B + LORA_W + LORA_A]
    gl = q[:, 3 * D_B + LORA_W + LORA_A:]
    w = -jax.nn.softplus(-(w0_ref[...] + _dot(jnp.tanh(wa), w2_ref[...]))) - 0.5
    a = jax.nn.sigmoid(a0_ref[...] + _dot(wa, a2_ref[...]))
    g_ref[...] = _dot(jax.nn.sigmoid(gl), g2_ref[...])
    kk = k * kkw_ref[...]
    sq_hi, sq_lo = _split(kk * kk)
    ssq = jnp.dot(sq_hi, seg_ref[...], preferred_element_type=F32) + jnp.dot(
        sq_lo, seg_ref[...], preferred_element_type=F32)
    kk = kk * lax.rsqrt(jnp.maximum(ssq, 1e-24))
    r_ref[...] = r
    k_ref[...] = k * (1.0 + (a - 1.0) * kaw_ref[...])
    v_ref[...] = v
    ld_ref[...] = -jnp.exp(w)
    kk_ref[...] = kk
    b_ref[...] = kk * a


def _ab_in(x, conv0, shift0, w_in, cw, cb, lng, lnb, mu, w0, w2p, a0, a2p, g2, kkw, kaw, seg, *, bq, tm):
    n = x.shape[0]
    hist = (CONV_A_WIDTH - 1) * bq
    nt = n // tm
    assert n % tm == 0 and tm % bq == 0 and (nt == 1 or tm >= hist)
    tok = jax.ShapeDtypeStruct((n, D_B), F32)
    ins = [x, w_in, cw, cb, lng, lnb, mu, w0, w2p, a0, a2p, g2, kkw, kaw, seg, conv0, shift0]
    in_specs = [_rows(tm, D_MODEL)] + [_full(a.shape) for a in ins[1:]]
    return pl.pallas_call(
        functools.partial(_ab_in_kernel, bq=bq, tm=tm),
        grid=(nt,),
        in_specs=in_specs,
        out_specs=[_rows(tm, D_B)] * 8 + [_full((hist, D_A)), _full((bq, D_BP))],
        out_shape=[tok] * 8 + [jax.ShapeDtypeStruct((hist, D_A), F32), jax.ShapeDtypeStruct((bq, D_BP), F32)],
        scratch_shapes=[pltpu.VMEM((hist + tm, D_A), F32), pltpu.VMEM((bq + tm, D_BP), F32)],
        compiler_params=_params(1), name="ab_in")(*ins)


def _wkv_kernel(r_ref, k_ref, v_ref, ld_ref, kk_ref, b_ref, g_ref, s0_ref, lnxg_ref, lnxb_ref, rk_ref,
                y_ref, so_ref, s_scr, *, c, sb, n_sq):
    @pl.when(pl.program_id(1) == 0)
    def _():
        s_scr[...] = s0_ref[...]

    row = lax.broadcasted_iota(jnp.int32, (c, c), 0)
    col = lax.broadcasted_iota(jnp.int32, (c, c), 1)
    incl = row >= col
    strict = row > col
    tri = jnp.where(incl, 1.0, 0.0).astype(BF16)
    eye = jnp.where(row == col, 1.0, 0.0).astype(F32)

    for q in range(sb):
        ld = ld_ref[q]
        ld_hi = ld.astype(BF16)
        ld_rem = ld - ld_hi.astype(F32)
        ld_mid = ld_rem.astype(BF16)
        ld_lo = (ld_rem - ld_mid.astype(F32)).astype(BF16)
        cum = (jnp.dot(tri, ld_hi, preferred_element_type=F32)
               + (jnp.dot(tri, ld_mid, preferred_element_type=F32)
                  + jnp.dot(tri, ld_lo, preferred_element_type=F32)))
        p_in = jnp.exp(cum)
        p_inv = jnp.exp(-cum)
        kk_t = kk_ref[q] * jnp.exp(cum - ld)
        r_t = r_ref[q] * p_in
        b_t = b_ref[q] * p_inv
        k_t = k_ref[q] * p_inv
        p_end = p_in[c - 1:c, :]
        r_all, k_all, v_all, g_all = r_ref[q], k_ref[q], v_ref[q], g_ref[q]

        for h in range(H_B):
            sl = slice(h * HEAD_B, (h + 1) * HEAD_B)
            left = jnp.concatenate([kk_t[:, sl], r_t[:, sl]], axis=0)
            right = jnp.concatenate([b_t[:, sl], k_t[:, sl]], axis=0)
            gram = _dot_hp(left, right, 1, 1)
            a_b = jnp.where(strict, gram[:c, :c], 0.0)
            a_k = jnp.where(strict, gram[:c, c:], 0.0)
            a_rb = jnp.where(incl, gram[c:, :c], 0.0)
            a_rk = jnp.where(incl, gram[c:, c:], 0.0)
            n_pow = -a_b
            t_inv = eye + n_pow
            for _ in range(n_sq):
                n_pow = _dot_hp(n_pow, n_pow)
                t_inv = t_inv + _dot_hp(t_inv, n_pow)
            s_prev = s_scr[q, h]
            v_h = v_all[:, sl]
            gs = _dot_hp(left, s_prev, 1, 1)
            u = _dot_hp(t_inv, -(gs[:c] + _dot_hp(a_k, v_h)))
            y = gs[c:] + _dot_hp(a_rb, u) + _dot_hp(a_rk, v_h)
            s_new = (s_prev + _dot_hp(u, b_t[:, sl], 0, 0) + _dot_hp(v_h, k_t[:, sl], 0, 0)) * p_end[:, sl]
            s_scr[q, h] = s_new
            so_ref[q, h] = s_new

            yc = y - jnp.mean(y, -1, keepdims=True)
            yn = yc * lax.rsqrt(jnp.mean(yc * yc, -1, keepdims=True) + GN_EPS)
            bonus = jnp.sum(r_all[:, sl] * k_all[:, sl] * rk_ref[:, sl], -1, keepdims=True) * v_h
            y_ref[q, :, sl] = (yn * lnxg_ref[:, sl] + lnxb_ref[:, sl] + bonus) * g_all[:, sl]


def _wkv(r, k, v, ld, kk, b, g, s0, lnxg, lnxb, rk, *, c, sb):
    nb, t, _ = r.shape
    assert t % c == 0 and nb % sb == 0
    n_sq = max(int(math.ceil(math.log2(c))) - 1, 0)
    seq = pl.BlockSpec((sb, c, D_B), lambda i, j: (i, j, 0))
    st = pl.BlockSpec((sb, H_B, HEAD_B, HEAD_B), lambda i, j: (i, 0, 0, 0))
    vec = pl.BlockSpec((1, D_B), lambda i, j: (0, 0))
    return pl.pallas_call(
        functools.partial(_wkv_kernel, c=c, sb=sb, n_sq=n_sq),
        grid=(nb // sb, t // c),
        in_specs=[seq] * 7 + [st, vec, vec, vec],
        out_specs=[seq, st],
        out_shape=[jax.ShapeDtypeStruct((nb, t, D_B), F32),
                   jax.ShapeDtypeStruct((nb, H_B, HEAD_B, HEAD_B), F32)],
        scratch_shapes=[pltpu.VMEM((sb, H_B, HEAD_B, HEAD_B), F32)],
        compiler_params=_params(2), name="wkv")(r, k, v, ld, kk, b, g, s0, lnxg, lnxb, rk)


def _ab_out_kernel(x_ref, ya_ref, yb_ref, w_ref, g_ref, b_ref, o_ref):
    out = _dot(ya_ref[...], w_ref[0:D_A, :]) + _dot(yb_ref[...], w_ref[D_A:, :])
    o_ref[...] = _layer_norm(ALPHA * x_ref[...] + out, g_ref[...], b_ref[...], LN_EPS)


def _ab_out(x, ya, yb, w_out, g, b, *, tm):
    n = x.shape[0]
    return pl.pallas_call(
        _ab_out_kernel, grid=(n // tm,),
        in_specs=[_rows(tm, D_MODEL), _rows(tm, D_A), _rows(tm, D_B), _full(w_out.shape), _full(g.shape),
                  _full(b.shape)],
        out_specs=_rows(tm, D_MODEL), out_shape=jax.ShapeDtypeStruct((n, D_MODEL), F32),
        compiler_params=_params(1), name="ab_out")(x, ya, yb, w_out, g, b)


def _s5_coef_kernel(lr_ref, li_ref, ldt_ref, br_ref, bi_ref, are_ref, aim_ref, bbr_ref, bbi_ref):
    lr, li = lr_ref[...], li_ref[...]
    dt = jnp.exp(ldt_ref[...])
    mag = jnp.exp(lr * dt)
    ab_re, ab_im = mag * jnp.cos(li * dt), mag * jnp.sin(li * dt)
    den = lr * lr + li * li
    nr, ni = ab_re - 1.0, ab_im
    f_re, f_im = (nr * lr + ni * li) / den, (ni * lr - nr * li) / den
    are_ref[...] = ab_re
    aim_ref[...] = ab_im
    for ch in range(C_GROUP):
        bbr_ref[ch] = f_re * br_ref[ch] - f_im * bi_ref[ch]
        bbi_ref[ch] = f_re * bi_ref[ch] + f_im * br_ref[ch]


def _s5_coef(lam_re, lam_im, log_dt, b_re, b_im):
    br = jnp.transpose(b_re, (2, 0, 1))
    bi = jnp.transpose(b_im, (2, 0, 1))
    gp = jax.ShapeDtypeStruct((G_C, P_C), F32)
    cgp = jax.ShapeDtypeStruct((C_GROUP, G_C, P_C), F32)
    return pl.pallas_call(_s5_coef_kernel, out_shape=[gp, gp, cgp, cgp], name="s5_coef")(
        lam_re, lam_im, log_dt[:, None], br, bi)


def _gelu_tanh(x):
    return 0.5 * x * (1.0 + jnp.tanh(math.sqrt(2.0 / math.pi) * (x + 0.044715 * (x * x * x))))


def _s5_kernel(x_ref, win_ref, bbd_ref, cre_ref, cim_ref, are_ref, aim_ref, dsk_ref, wout_ref, lng_ref,
               lnb_ref, hre0_ref, him0_ref, o_ref, hreo_ref, himo_ref, bre_scr, bim_scr, hre_scr, him_scr,
               *, bq, tm, lb):
    @pl.when(pl.program_id(0) == 0)
    def _():
        hre_scr[...] = hre0_ref[...]
        him_scr[...] = him0_ref[...]

    x = x_ref[...]
    u = _dot(x, win_ref[...])
    ub = u.astype(BF16)
    for j in range(N_SLAB):
        bu = jnp.dot(ub[:, j * SLAB:(j + 1) * SLAB], bbd_ref[j], preferred_element_type=F32)
        bre_scr[:, j * SSM_SLAB:(j + 1) * SSM_SLAB] = bu[:, :SSM_SLAB]
        bim_scr[:, j * SSM_SLAB:(j + 1) * SSM_SLAB] = bu[:, SSM_SLAB:]

    for blk in range(D_SSM // lb):
        cs = slice(blk * lb, (blk + 1) * lb)
        a_re = are_ref[:, cs]
        a_im = aim_ref[:, cs]

        def step(t, carry, cs=cs, a_re=a_re, a_im=a_im):
            h_re, h_im = carry
            rows = pl.ds(pl.multiple_of(t * bq, bq), bq)
            n_re = a_re * h_re - a_im * h_im + bre_scr[rows, cs]
            n_im = a_re * h_im + a_im * h_re + bim_scr[rows, cs]
            bre_scr[rows, cs] = n_re
            bim_scr[rows, cs] = n_im
            return n_re, n_im

        h_re, h_im = lax.fori_loop(0, tm // bq, step, (hre_scr[:, cs], him_scr[:, cs]))
        hre_scr[:, cs] = h_re
        him_scr[:, cs] = h_im
    hreo_ref[...] = hre_scr[...]
    himo_ref[...] = him_scr[...]

    ys = []
    for j in range(N_SLAB):
        cs = slice(j * SSM_SLAB, (j + 1) * SSM_SLAB)
        ys.append(_dot(bre_scr[:, cs], cre_ref[j]) - _dot(bim_scr[:, cs], cim_ref[j]))
    y = jnp.concatenate(ys, axis=1) + dsk_ref[...] * u
    o = _dot(_gelu_tanh(y), wout_ref[...])
    out = o[:, :D_MODEL] * jax.nn.sigmoid(o[:, D_MODEL:])
    o_ref[...] = _layer_norm(ALPHA * x + out, lng_ref[...], lnb_ref[...], LN_EPS)


def _s5(x, h_re0, h_im0, w_in, bbd, cre, cim, a_re, a_im, dsk, w_out, lng, lnb, *, bq, tm, lb):
    n = x.shape[0]
    assert n % tm == 0 and tm % bq == 0
    ins = [x, w_in, bbd, cre, cim, a_re, a_im, dsk, w_out, lng, lnb, h_re0, h_im0]
    st = jax.ShapeDtypeStruct((bq, D_SSM), F32)
    return pl.pallas_call(
        functools.partial(_s5_kernel, bq=bq, tm=tm, lb=lb),
        grid=(n // tm,),
        in_specs=[_rows(tm, D_MODEL)] + [_full(a.shape) for a in ins[1:]],
        out_specs=[_rows(tm, D_MODEL), _full((bq, D_SSM)), _full((bq, D_SSM))],
        out_shape=[jax.ShapeDtypeStruct((n, D_MODEL), F32), st, st],
        scratch_shapes=[pltpu.VMEM((tm, D_SSM), F32), pltpu.VMEM((tm, D_SSM), F32),
                        pltpu.VMEM((bq, D_SSM), F32), pltpu.VMEM((bq, D_SSM), F32)],
        compiler_params=_params(1), name="s5")(*ins)


def _ffn_kernel(x_ref, wup_ref, cw_ref, cb_ref, wdn_ref, lng_ref, lnb_ref, st0_ref, o_ref, sto_ref,
                carry_scr, *, bq, tm):
    @pl.when(pl.program_id(0) == 0)
    def _():
        carry_scr[...] = st0_ref[...]

    x = x_ref[...]
    xb = x.astype(BF16)
    acc = jnp.zeros((tm, D_MODEL), F32)
    for ci in range(D_FF // FFN_CHUNK):
        cs = slice(ci * FFN_CHUNK, (ci + 1) * FFN_CHUNK)
        gs = slice(D_FF + ci * FFN_CHUNK, D_FF + (ci + 1) * FFN_CHUNK)
        h = jnp.dot(xb, wup_ref[:, cs], preferred_element_type=F32)
        gate = jnp.dot(xb, wup_ref[:, gs], preferred_element_type=F32)
        old = carry_scr[:, cs]
        h_m1 = jnp.concatenate([old[bq:], h[:tm - bq]], axis=0)
        h_m2 = jnp.concatenate([old, h[:tm - 2 * bq]], axis=0)
        cv = cw_ref[0:1, cs] * h_m2 + cw_ref[1:2, cs] * h_m1 + cw_ref[2:3, cs] * h + cb_ref[:, cs]
        act = cv * jax.nn.sigmoid(cv) * gate
        acc = acc + jnp.dot(act.astype(BF16), wdn_ref[cs, :], preferred_element_type=F32)
        carry_scr[:, cs] = h[tm - 2 * bq:]
    o_ref[...] = _layer_norm(ALPHA * x + acc, lng_ref[...], lnb_ref[...], LN_EPS)
    sto_ref[...] = carry_scr[...]


def _ffn(x, st0, w_up, cw, cb, w_dn, lng, lnb, *, bq, tm):
    n = x.shape[0]
    assert n % tm == 0 and tm >= 2 * bq and D_FF % FFN_CHUNK == 0
    ins = [x, w_up, cw, cb, w_dn, lng, lnb, st0]
    st = jax.ShapeDtypeStruct(st0.shape, F32)
    return pl.pallas_call(
        functools.partial(_ffn_kernel, bq=bq, tm=tm),
        grid=(n // tm,),
        in_specs=[_rows(tm, D_MODEL)] + [_full(a.shape) for a in ins[1:]],
        out_specs=[_rows(tm, D_MODEL), _full(st0.shape)],
        out_shape=[jax.ShapeDtypeStruct((n, D_MODEL), F32), st],
        scratch_shapes=[pltpu.VMEM(st0.shape, F32)],
        compiler_params=_params(1), name="ffn")(*ins)


def _trunk(x, st_conv, st_shift, st_wkv, st_re, st_im, st_ffn, pr, *, bq, t_len, tiles):
    n_ab, n_c = (DEPTH + 1) // 2, DEPTH // 2
    hist = CONV_A_WIDTH - 1
    c = min(WKV_CHUNK, 8 * ((t_len + 7) // 8))
    t_pad = c * ((t_len + c - 1) // c)
    o_conv, o_shift, o_wkv, o_re, o_im, o_ffn = [], [], [], [], [], []

    def to_seq(z):
        z = jnp.transpose(z.reshape(t_len, bq, D_B), (1, 0, 2))
        return jnp.pad(z, ((0, 0), (0, t_pad - t_len), (0, 0)))

    for i in range(DEPTH):
        j = i // 2
        if i % 2 == 0:
            conv0 = jnp.transpose(st_conv[:, j], (1, 0, 2)).reshape(hist * bq, D_A)
            outs = _ab_in(x, conv0, st_shift[:, j], pr["w_in_ab"][j], pr["conv_a_w"][j], pr["conv_a_b"][j],
                          pr["ln_a_g"][j], pr["ln_a_b"][j], pr["mu_b"][j], pr["w0_b"][j], pr["w2_pad"][j],
                          pr["a0_b"][j], pr["a2_pad"][j], pr["g2_b"][j], pr["k_k_b"][j], pr["k_a_b"][j],
                          pr["seg"], bq=bq, tm=tiles["ab_in"])
            ya, r, k, v, ld, kk, bv, g, conv_n, shift_n = outs
            yb_seq, wkv_n = _wkv(*(to_seq(z) for z in (r, k, v, ld, kk, bv, g)), st_wkv[:, j],
                                 pr["ln_x_g"][j], pr["ln_x_b"][j], pr["r_k_b"][j], c=c, sb=tiles["wkv_sb"])
            yb = jnp.transpose(yb_seq[:, :t_len], (1, 0, 2)).reshape(t_len * bq, D_B)
            x = _ab_out(x, ya, yb, pr["w_out_ab"][j], pr["ln1_g"][i], pr["ln1_b"][i], tm=tiles["ab_out"])
            o_conv.append(jnp.transpose(conv_n.reshape(hist, bq, D_A), (1, 0, 2)))
            o_shift.append(shift_n)
            o_wkv.append(wkv_n)
        else:
            x, h_re, h_im = _s5(x, st_re[:, j].reshape(bq, D_SSM), st_im[:, j].reshape(bq, D_SSM),
                                pr["w_in_c"][j], pr["bbd"][j], pr["cre"][j], pr["cim"][j], pr["a_re"][j],
                                pr["a_im"][j], pr["d_skip"][j], pr["w_out_c"][j], pr["ln1_g"][i],
                                pr["ln1_b"][i], bq=bq, tm=tiles["s5"], lb=tiles["s5_lb"])
            o_re.append(h_re.reshape(bq, G_C, P_C))
            o_im.append(h_im.reshape(bq, G_C, P_C))
        ffn0 = jnp.transpose(st_ffn[:, i], (1, 0, 2)).reshape((FFN_CONV_WIDTH - 1) * bq, D_FF)
        x, ffn_n = _ffn(x, ffn0, pr["w_up"][i], pr["conv_f_w"][i], pr["conv_f_b"][i], pr["w_down"][i],
                        pr["ln2_g"][i], pr["ln2_b"][i], bq=bq, tm=tiles["ffn"])
        o_ffn.append(jnp.transpose(ffn_n.reshape(FFN_CONV_WIDTH - 1, bq, D_FF), (1, 0, 2)))
    stack = lambda xs: jnp.stack(xs, 1)
    return x, stack(o_conv), stack(o_shift), stack(o_wkv), stack(o_re), stack(o_im), stack(o_ffn)


def _prepare(w_in_ab, conv_a_w, conv_a_b, ln_a_g, ln_a_b, mu_b, w0_b, w2_b, a0_b, a2_b, g2_b, k_k_b, k_a_b,
             r_k_b, ln_x_g, ln_x_b, w_out_ab, w_in_c, lam_re, lam_im, log_dt, b_re, b_im, c_re, c_im, d_skip,
             w_out_c, w_up, conv_f_w, conv_f_b, w_down, ln1_g, ln1_b, ln2_g, ln2_b):
    n_ab, n_c = w_in_ab.shape[0], w_in_c.shape[0]
    row = lambda a: a.reshape(a.shape[0], 1, -1)
    zeros = jnp.zeros((n_ab, LORA_W, D_B), F32)
    eye = jnp.eye(G_SLAB, dtype=F32)
    coefs = [_s5_coef(lam_re[j], lam_im[j], log_dt[j], b_re[j], b_im[j]) for j in range(n_c)]

    def drive(bb):
        bb = bb.reshape(C_GROUP, N_SLAB, G_SLAB, P_C)
        return jnp.einsum("csgp,gh->sgchp", bb, eye).reshape(N_SLAB, SLAB, SSM_SLAB)

    def readout(cc):
        cc = cc.reshape(N_SLAB, G_SLAB, C_GROUP, P_C)
        return jnp.einsum("sgcp,gh->shpgc", cc, eye).reshape(N_SLAB, SSM_SLAB, SLAB)

    return {
        "w_in_ab": w_in_ab.astype(BF16), "conv_a_w": conv_a_w, "conv_a_b": row(conv_a_b),
        "ln_a_g": row(ln_a_g), "ln_a_b": row(ln_a_b), "mu_b": row(mu_b), "w0_b": row(w0_b),
        "w2_pad": jnp.concatenate([w2_b, zeros], axis=1).astype(BF16), "a0_b": row(a0_b),
        "a2_pad": jnp.concatenate([zeros, a2_b], axis=1).astype(BF16), "g2_b": g2_b.astype(BF16),
        "k_k_b": row(k_k_b), "k_a_b": row(k_a_b), "r_k_b": row(r_k_b), "ln_x_g": row(ln_x_g),
        "ln_x_b": row(ln_x_b), "w_out_ab": w_out_ab.astype(BF16),
        "seg": jnp.kron(jnp.eye(H_B, dtype=F32), jnp.ones((HEAD_B, HEAD_B), F32)).astype(BF16),
        "w_in_c": w_in_c.astype(BF16),
        "bbd": jnp.stack([jnp.concatenate([drive(cf[2]), drive(cf[3])], axis=-1) for cf in coefs]).astype(BF16),
        "cre": jnp.stack([readout(c_re[j]) for j in range(n_c)]).astype(BF16),
        "cim": jnp.stack([readout(c_im[j]) for j in range(n_c)]).astype(BF16),
        "a_re": jnp.stack([cf[0].reshape(1, D_SSM) for cf in coefs]),
        "a_im": jnp.stack([cf[1].reshape(1, D_SSM) for cf in coefs]),
        "d_skip": row(d_skip), "w_out_c": w_out_c.astype(BF16),
        "w_up": w_up.astype(BF16), "conv_f_w": conv_f_w, "conv_f_b": row(conv_f_b),
        "w_down": w_down.astype(BF16),
        "ln1_g": row(ln1_g), "ln1_b": row(ln1_b), "ln2_g": row(ln2_g), "ln2_b": row(ln2_b),
    }


PROMPT_TILES = {"ab_in": 256, "wkv_sb": 1, "ab_out": 512, "s5": 256, "s5_lb": 1024, "ffn": 512}
SAMPLE_TILES = {"ab_in": 512, "wkv_sb": 8, "ab_out": 512, "s5": 512, "s5_lb": 128, "ffn": 512}


def _run_group(x, states, pr, tiles):
    bq, t_len, _ = x.shape
    xt = jnp.transpose(x, (1, 0, 2)).reshape(t_len * bq, D_MODEL)
    y, *new_states = _trunk(xt, *states, pr, bq=bq, t_len=t_len, tiles=tiles)
    return (jnp.transpose(y.reshape(t_len, bq, D_MODEL), (1, 0, 2)), *new_states)


def kernel(x_prompt, x_sample, state_conv_a, state_shift_b, state_wkv_b, state_ssm_re, state_ssm_im,
           state_conv_ffn, w_in_ab, conv_a_w, conv_a_b, ln_a_g, ln_a_b, mu_b, w0_b, w2_b, a0_b, a2_b, g2_b,
           k_k_b, k_a_b, r_k_b, ln_x_g, ln_x_b, w_out_ab, w_in_c, lam_re, lam_im, log_dt, b_re, b_im, c_re,
           c_im, d_skip, w_out_c, w_up, conv_f_w, conv_f_b, w_down, ln1_g, ln1_b, ln2_g, ln2_b):
    pr = _prepare(w_in_ab, conv_a_w, conv_a_b, ln_a_g, ln_a_b, mu_b, w0_b, w2_b, a0_b, a2_b, g2_b, k_k_b,
                  k_a_b, r_k_b.reshape(r_k_b.shape[0], D_B), ln_x_g, ln_x_b, w_out_ab, w_in_c, lam_re, lam_im,
                  log_dt, b_re, b_im, c_re, c_im, d_skip, w_out_c, w_up, conv_f_w, conv_f_b, w_down, ln1_g,
                  ln1_b, ln2_g, ln2_b)
    sample_states = (state_conv_a, state_shift_b, state_wkv_b, state_ssm_re, state_ssm_im, state_conv_ffn)
    bp = x_prompt.shape[0]
    prompt_states = tuple(jnp.zeros((bp,) + s.shape[1:], s.dtype) for s in sample_states)
    p = _run_group(x_prompt, prompt_states, pr, PROMPT_TILES)
    s = _run_group(x_sample, sample_states, pr, SAMPLE_TILES)
    return (p[0], s[0], p[1], s[1], p[2], s[2], p[3], s[3], p[4], s[4], p[5], s[5], p[6], s[6])
```

```python
import functools
import math

import jax
import jax.numpy as jnp
from jax import lax
from jax.experimental import pallas as pl
from jax.experimental.pallas import tpu as pltpu

F32 = jnp.float32
BF16 = jnp.bfloat16

D_MODEL = 1024
DEPTH = 4
D_A = 512
D_B = 512
HEAD_B = 64
H_B = D_B // HEAD_B
LORA_W = 64
LORA_A = 64
LORA_G = 128
D_BP = 3 * D_B + LORA_W + LORA_A + LORA_G
D_IN_AB = 2 * D_A + D_BP
CONV_A_WIDTH = 31
C_GROUP = 16
G_C = D_MODEL // C_GROUP
P_C = 64
D_SSM = G_C * P_C
SLAB = 128
N_SLAB = D_MODEL // SLAB
G_SLAB = SLAB // C_GROUP
SSM_SLAB = G_SLAB * P_C
D_FF = 2816
FFN_CONV_WIDTH = 3
ALPHA = (2 * DEPTH) ** 0.25
LN_EPS = 1e-5
GN_EPS = HEAD_B * 1e-5

VMEM_LIMIT_BYTES = 56 * 1024 * 1024
FFN_CHUNK = 256
WKV_CHUNK = 64


def _params(n_grid):
    return pltpu.CompilerParams(dimension_semantics=("arbitrary",) * n_grid,
                                vmem_limit_bytes=VMEM_LIMIT_BYTES)


def _full(shape):
    nd = len(shape)
    return pl.BlockSpec(shape, lambda *_: (0,) * nd)


def _rows(tm, width):
    return pl.BlockSpec((tm, width), lambda i: (i, 0))


def _dot(a, b):
    return jnp.dot(a.astype(BF16), b.astype(BF16), preferred_element_type=F32)


def _split(a):
    hi = a.astype(BF16)
    lo = (a - hi.astype(F32)).astype(BF16)
    return hi, lo


def _dg(a, b, ca, cb):
    return lax.dot_general(a, b, (((ca,), (cb,)), ((), ())), preferred_element_type=F32)


def _dot_hp(a, b, ca=1, cb=0):
    ah, al = _split(a)
    bh, bl = _split(b)
    return _dg(ah, bh, ca, cb) + (_dg(ah, bl, ca, cb) + _dg(al, bh, ca, cb))


def _layer_norm(x, g, b, eps):
    xc = x - jnp.mean(x, -1, keepdims=True)
    var = jnp.mean(xc * xc, -1, keepdims=True)
    return xc * lax.rsqrt(var + eps) * g + b


def _ab_in_kernel(x_ref, w_ref, cw_ref, cb_ref, lng_ref, lnb_ref, mu_ref, w0_ref, w2_ref, a0_ref,
                  a2_ref, g2_ref, kkw_ref, kaw_ref, seg_ref, conv0_ref, shift0_ref,
                  ya_ref, r_ref, k_ref, v_ref, ld_ref, kk_ref, b_ref, g_ref, convo_ref, shifto_ref,
                  ext_scr, sh_scr, *, bq, tm):
    hist = (CONV_A_WIDTH - 1) * bq

    @pl.when(pl.program_id(0) == 0)
    def _():
        ext_scr[0:hist, :] = conv0_ref[...]
        sh_scr[0:bq, :] = shift0_ref[...]

    p = _dot(x_ref[...], w_ref[...])

    ext_scr[hist:hist + tm, :] = p[:, :D_A] * jax.nn.sigmoid(p[:, D_A:2 * D_A])
    acc = jnp.zeros((tm, D_A), F32) + cb_ref[...]
    for j in range(CONV_A_WIDTH):
        acc = acc + cw_ref[j:j + 1, :] * ext_scr[j * bq:j * bq + tm, :]
    ya = _layer_norm(acc, lng_ref[...], lnb_ref[...], LN_EPS)
    ya_ref[...] = ya * jax.nn.sigmoid(ya)
    new_hist = ext_scr[tm:tm + hist, :]
    convo_ref[...] = new_hist
    ext_scr[0:hist, :] = new_hist

    pb = p[:, 2 * D_A:]
    sh_scr[bq:bq + tm, :] = pb
    prev = sh_scr[0:tm, :]
    last = pb[tm - bq:, :]
    sh_scr[0:bq, :] = last
    shifto_ref[...] = last
    q = pb + (prev - pb) * mu_ref[...]
    r = q[:, :D_B]
    k = q[:, D_B:2 * D_B]
    v = q[:, 2 * D_B:3 * D_B]
    wa = q[:, 3 * D_B:3 * D_B + LORA_W + LORA_A]
    gl = q[:, 3 * D_B + LORA_W + LORA_A:]
    w = -jax.nn.softplus(-(w0_ref[...] + _dot(jnp.tanh(wa), w2_ref[...]))) - 0.5
    a = jax.nn.sigmoid(a0_ref[...] + _dot(wa, a2_ref[...]))
    g_ref[...] = _dot(jax.nn.sigmoid(gl), g2_ref[...])
    kk = k * kkw_ref[...]
    sq_hi, sq_lo = _split(kk * kk)
    ssq = jnp.dot(sq_hi, seg_ref[...], preferred_element_type=F32) + jnp.dot(
        sq_lo, seg_ref[...], preferred_element_type=F32)
    kk = kk * lax.rsqrt(jnp.maximum(ssq, 1e-24))
    r_ref[...] = r
    k_ref[...] = k * (1.0 + (a - 1.0) * kaw_ref[...])
    v_ref[...] = v
    ld_ref[...] = -jnp.exp(w)
    kk_ref[...] = kk
    b_ref[...] = kk * a


def _ab_in(x, conv0, shift0, w_in, cw, cb, lng, lnb, mu, w0, w2p, a0, a2p, g2, kkw, kaw, seg, *, bq, tm):
    n = x.shape[0]
    hist = (CONV_A_WIDTH - 1) * bq
    nt = n // tm
    assert n % tm == 0 and tm % bq == 0 and (nt == 1 or tm >= hist)
    tok = jax.ShapeDtypeStruct((n, D_B), F32)
    ins = [x, w_in, cw, cb, lng, lnb, mu, w0, w2p, a0, a2p, g2, kkw, kaw, seg, conv0, shift0]
    in_specs = [_rows(tm, D_MODEL)] + [_full(a.shape) for a in ins[1:]]
    return pl.pallas_call(
        functools.partial(_ab_in_kernel, bq=bq, tm=tm),
        grid=(nt,),
        in_specs=in_specs,
        out_specs=[_rows(tm, D_B)] * 8 + [_full((hist, D_A)), _full((bq, D_BP))],
        out_shape=[tok] * 8 + [jax.ShapeDtypeStruct((hist, D_A), F32), jax.ShapeDtypeStruct((bq, D_BP), F32)],
        scratch_shapes=[pltpu.VMEM((hist + tm, D_A), F32), pltpu.VMEM((bq + tm, D_BP), F32)],
        compiler_params=_params(1), name="ab_in")(*ins)


def _wkv_kernel(r_ref, k_ref, v_ref, ld_ref, kk_ref, b_ref, g_ref, s0_ref, lnxg_ref, lnxb_ref, rk_ref,
                y_ref, so_ref, s_scr, *, c, sb, n_sq):
    @pl.when(pl.program_id(1) == 0)
    def _():
        s_scr[...] = s0_ref[...]

    row = lax.broadcasted_iota(jnp.int32, (c, c), 0)
    col = lax.broadcasted_iota(jnp.int32, (c, c), 1)
    incl = row >= col
    strict = row > col
    tri = jnp.where(incl, 1.0, 0.0).astype(BF16)
    eye = jnp.where(row == col, 1.0, 0.0).astype(F32)

    left, right, v_hd, b_hd, k_hd, p_end, s_prev, tail = [], [], [], [], [], [], [], []
    for q in range(sb):
        ld = ld_ref[q]
        ld_hi = ld.astype(BF16)
        ld_rem = ld - ld_hi.astype(F32)
        ld_mid = ld_rem.astype(BF16)
        ld_lo = (ld_rem - ld_mid.astype(F32)).astype(BF16)
        cum = (jnp.dot(tri, ld_hi, preferred_element_type=F32)
               + (jnp.dot(tri, ld_mid, preferred_element_type=F32)
                  + jnp.dot(tri, ld_lo, preferred_element_type=F32)))
        p_in = jnp.exp(cum)
        p_inv = jnp.exp(-cum)
        kk_t = kk_ref[q] * jnp.exp(cum - ld)
        r_t = r_ref[q] * p_in
        b_t = b_ref[q] * p_inv
        k_t = k_ref[q] * p_inv
        r_all, k_all, v_all, g_all = r_ref[q], k_ref[q], v_ref[q], g_ref[q]
        for h in range(H_B):
            sl = slice(h * HEAD_B, (h + 1) * HEAD_B)
            left.append(jnp.concatenate([kk_t[:, sl], r_t[:, sl]], axis=0))
            right.append(jnp.concatenate([b_t[:, sl], k_t[:, sl]], axis=0))
            v_hd.append(v_all[:, sl])
            b_hd.append(b_t[:, sl])
            k_hd.append(k_t[:, sl])
            p_end.append(p_in[c - 1:c, sl])
            s_prev.append(s_scr[q, h])
            bonus = jnp.sum(r_all[:, sl] * k_all[:, sl] * rk_ref[:, sl], -1, keepdims=True) * v_all[:, sl]
            tail.append((q, sl, bonus, g_all[:, sl]))

    pairs = range(sb * H_B)
    gram = [_dot_hp(left[i], right[i], 1, 1) for i in pairs]
    a_k = [jnp.where(strict, gram[i][:c, c:], 0.0) for i in pairs]
    a_rb = [jnp.where(incl, gram[i][c:, :c], 0.0) for i in pairs]
    a_rk = [jnp.where(incl, gram[i][c:, c:], 0.0) for i in pairs]
    n_pow = [jnp.where(strict, -gram[i][:c, :c], 0.0) for i in pairs]
    t_inv = [eye + n_pow[i] for i in pairs]
    gs = [_dot_hp(left[i], s_prev[i], 1, 1) for i in pairs]
    rhs = [gs[i][:c] + _dot_hp(a_k[i], v_hd[i]) for i in pairs]
    y_v = [gs[i][c:] + _dot_hp(a_rk[i], v_hd[i]) for i in pairs]
    s_v = [s_prev[i] + _dot_hp(v_hd[i], k_hd[i], 0, 0) for i in pairs]
    for _ in range(n_sq):
        n_pow = [_dot_hp(n_pow[i], n_pow[i]) for i in pairs]
        t_inv = [t_inv[i] + _dot_hp(t_inv[i], n_pow[i]) for i in pairs]
    u = [-_dot_hp(t_inv[i], rhs[i]) for i in pairs]
    y = [y_v[i] + _dot_hp(a_rb[i], u[i]) for i in pairs]
    s_new = [(s_v[i] + _dot_hp(u[i], b_hd[i], 0, 0)) * p_end[i] for i in pairs]
    for i in pairs:
        q, sl, bonus, g_h = tail[i]
        s_scr[q, i % H_B] = s_new[i]
        so_ref[q, i % H_B] = s_new[i]
        yc = y[i] - jnp.mean(y[i], -1, keepdims=True)
        yn = yc * lax.rsqrt(jnp.mean(yc * yc, -1, keepdims=True) + GN_EPS)
        y_ref[q, :, sl] = (yn * lnxg_ref[:, sl] + lnxb_ref[:, sl] + bonus) * g_h


def _wkv(r, k, v, ld, kk, b, g, s0, lnxg, lnxb, rk, *, c, sb):
    nb, t, _ = r.shape
    assert t % c == 0 and nb % sb == 0
    n_sq = max(int(math.ceil(math.log2(c))) - 1, 0)
    seq = pl.BlockSpec((sb, c, D_B), lambda i, j: (i, j, 0))
    st = pl.BlockSpec((sb, H_B, HEAD_B, HEAD_B), lambda i, j: (i, 0, 0, 0))
    vec = pl.BlockSpec((1, D_B), lambda i, j: (0, 0))
    return pl.pallas_call(
        functools.partial(_wkv_kernel, c=c, sb=sb, n_sq=n_sq),
        grid=(nb // sb, t // c),
        in_specs=[seq] * 7 + [st, vec, vec, vec],
        out_specs=[seq, st],
        out_shape=[jax.ShapeDtypeStruct((nb, t, D_B), F32),
                   jax.ShapeDtypeStruct((nb, H_B, HEAD_B, HEAD_B), F32)],
        scratch_shapes=[pltpu.VMEM((sb, H_B, HEAD_B, HEAD_B), F32)],
        compiler_params=_params(2), name="wkv")(r, k, v, ld, kk, b, g, s0, lnxg, lnxb, rk)


def _ab_out_kernel(x_ref, ya_ref, yb_ref, w_ref, g_ref, b_ref, o_ref):
    out = _dot(ya_ref[...], w_ref[0:D_A, :]) + _dot(yb_ref[...], w_ref[D_A:, :])
    o_ref[...] = _layer_norm(ALPHA * x_ref[...] + out, g_ref[...], b_ref[...], LN_EPS)


def _ab_out(x, ya, yb, w_out, g, b, *, tm):
    n = x.shape[0]
    return pl.pallas_call(
        _ab_out_kernel, grid=(n // tm,),
        in_specs=[_rows(tm, D_MODEL), _rows(tm, D_A), _rows(tm, D_B), _full(w_out.shape), _full(g.shape),
                  _full(b.shape)],
        out_specs=_rows(tm, D_MODEL), out_shape=jax.ShapeDtypeStruct((n, D_MODEL), F32),
        compiler_params=_params(1), name="ab_out")(x, ya, yb, w_out, g, b)


def _s5_coef_kernel(lr_ref, li_ref, ldt_ref, br_ref, bi_ref, are_ref, aim_ref, bbr_ref, bbi_ref):
    lr, li = lr_ref[...], li_ref[...]
    dt = jnp.exp(ldt_ref[...])
    mag = jnp.exp(lr * dt)
    ab_re, ab_im = mag * jnp.cos(li * dt), mag * jnp.sin(li * dt)
    den = lr * lr + li * li
    nr, ni = ab_re - 1.0, ab_im
    f_re, f_im = (nr * lr + ni * li) / den, (ni * lr - nr * li) / den
    are_ref[...] = ab_re
    aim_ref[...] = ab_im
    for ch in range(C_GROUP):
        bbr_ref[ch] = f_re * br_ref[ch] - f_im * bi_ref[ch]
        bbi_ref[ch] = f_re * bi_ref[ch] + f_im * br_ref[ch]


def _s5_coef(lam_re, lam_im, log_dt, b_re, b_im):
    br = jnp.transpose(b_re, (2, 0, 1))
    bi = jnp.transpose(b_im, (2, 0, 1))
    gp = jax.ShapeDtypeStruct((G_C, P_C), F32)
    cgp = jax.ShapeDtypeStruct((C_GROUP, G_C, P_C), F32)
    return pl.pallas_call(_s5_coef_kernel, out_shape=[gp, gp, cgp, cgp], name="s5_coef")(
        lam_re, lam_im, log_dt[:, None], br, bi)


def _gelu_tanh(x):
    return 0.5 * x * (1.0 + jnp.tanh(math.sqrt(2.0 / math.pi) * (x + 0.044715 * (x * x * x))))


def _s5_kernel(x_ref, win_ref, bbd_ref, cre_ref, cim_ref, are_ref, aim_ref, dsk_ref, wout_ref, lng_ref,
               lnb_ref, hre0_ref, him0_ref, o_ref, hreo_ref, himo_ref, bre_scr, bim_scr, hre_scr, him_scr,
               *, bq, tm, lb):
    @pl.when(pl.program_id(0) == 0)
    def _():
        hre_scr[...] = hre0_ref[...]
        him_scr[...] = him0_ref[...]

    x = x_ref[...]
    u = _dot(x, win_ref[...])
    ub = u.astype(BF16)
    for j in range(N_SLAB):
        bu = jnp.dot(ub[:, j * SLAB:(j + 1) * SLAB], bbd_ref[j], preferred_element_type=F32)
        bre_scr[:, j * SSM_SLAB:(j + 1) * SSM_SLAB] = bu[:, :SSM_SLAB]
        bim_scr[:, j * SSM_SLAB:(j + 1) * SSM_SLAB] = bu[:, SSM_SLAB:]

    for blk in range(D_SSM // lb):
        cs = slice(blk * lb, (blk + 1) * lb)
        a_re = are_ref[:, cs]
        a_im = aim_ref[:, cs]

        def step(t, carry, cs=cs, a_re=a_re, a_im=a_im):
            h_re, h_im = carry
            rows = pl.ds(pl.multiple_of(t * bq, bq), bq)
            n_re = a_re * h_re - a_im * h_im + bre_scr[rows, cs]
            n_im = a_re * h_im + a_im * h_re + bim_scr[rows, cs]
            bre_scr[rows, cs] = n_re
            bim_scr[rows, cs] = n_im
            return n_re, n_im

        h_re, h_im = lax.fori_loop(0, tm // bq, step, (hre_scr[:, cs], him_scr[:, cs]))
        hre_scr[:, cs] = h_re
        him_scr[:, cs] = h_im
    hreo_ref[...] = hre_scr[...]
    himo_ref[...] = him_scr[...]

    ys = []
    for j in range(N_SLAB):
        cs = slice(j * SSM_SLAB, (j + 1) * SSM_SLAB)
        ys.append(_dot(bre_scr[:, cs], cre_ref[j]) - _dot(bim_scr[:, cs], cim_ref[j]))
    y = jnp.concatenate(ys, axis=1) + dsk_ref[...] * u
    o = _dot(_gelu_tanh(y), wout_ref[...])
    out = o[:, :D_MODEL] * jax.nn.sigmoid(o[:, D_MODEL:])
    o_ref[...] = _layer_norm(ALPHA * x + out, lng_ref[...], lnb_ref[...], LN_EPS)


def _s5(x, h_re0, h_im0, w_in, bbd, cre, cim, a_re, a_im, dsk, w_out, lng, lnb, *, bq, tm, lb):
    n = x.shape[0]
    assert n % tm == 0 and tm % bq == 0
    ins = [x, w_in, bbd, cre, cim, a_re, a_im, dsk, w_out, lng, lnb, h_re0, h_im0]
    st = jax.ShapeDtypeStruct((bq, D_SSM), F32)
    return pl.pallas_call(
        functools.partial(_s5_kernel, bq=bq, tm=tm, lb=lb),
        grid=(n // tm,),
        in_specs=[_rows(tm, D_MODEL)] + [_full(a.shape) for a in ins[1:]],
        out_specs=[_rows(tm, D_MODEL), _full((bq, D_SSM)), _full((bq, D_SSM))],
        out_shape=[jax.ShapeDtypeStruct((n, D_MODEL), F32), st, st],
        scratch_shapes=[pltpu.VMEM((tm, D_SSM), F32), pltpu.VMEM((tm, D_SSM), F32),
                        pltpu.VMEM((bq, D_SSM), F32), pltpu.VMEM((bq, D_SSM), F32)],
        compiler_params=_params(1), name="s5")(*ins)


def _ffn_kernel(x_ref, wup_ref, cw_ref, cb_ref, wdn_ref, lng_ref, lnb_ref, st0_ref, o_ref, sto_ref,
                carry_scr, *, bq, tm):
    @pl.when(pl.program_id(0) == 0)
    def _():
        carry_scr[...] = st0_ref[...]

    x = x_ref[...]
    xb = x.astype(BF16)
    acc = jnp.zeros((tm, D_MODEL), F32)
    for ci in range(D_FF // FFN_CHUNK):
        cs = slice(ci * FFN_CHUNK, (ci + 1) * FFN_CHUNK)
        gs = slice(D_FF + ci * FFN_CHUNK, D_FF + (ci + 1) * FFN_CHUNK)
        h = jnp.dot(xb, wup_ref[:, cs], preferred_element_type=F32)
        gate = jnp.dot(xb, wup_ref[:, gs], preferred_element_type=F32)
        old = carry_scr[:, cs]
        h_m1 = jnp.concatenate([old[bq:], h[:tm - bq]], axis=0)
        h_m2 = jnp.concatenate([old, h[:tm - 2 * bq]], axis=0)
        cv = cw_ref[0:1, cs] * h_m2 + cw_ref[1:2, cs] * h_m1 + cw_ref[2:3, cs] * h + cb_ref[:, cs]
        act = cv * jax.nn.sigmoid(cv) * gate
        acc = acc + jnp.dot(act.astype(BF16), wdn_ref[cs, :], preferred_element_type=F32)
        carry_scr[:, cs] = h[tm - 2 * bq:]
    o_ref[...] = _layer_norm(ALPHA * x + acc, lng_ref[...], lnb_ref[...], LN_EPS)
    sto_ref[...] = carry_scr[...]


def _ffn(x, st0, w_up, cw, cb, w_dn, lng, lnb, *, bq, tm):
    n = x.shape[0]
    assert n % tm == 0 and tm >= 2 * bq and D_FF % FFN_CHUNK == 0
    ins = [x, w_up, cw, cb, w_dn, lng, lnb, st0]
    st = jax.ShapeDtypeStruct(st0.shape, F32)
    return pl.pallas_call(
        functools.partial(_ffn_kernel, bq=bq, tm=tm),
        grid=(n // tm,),
        in_specs=[_rows(tm, D_MODEL)] + [_full(a.shape) for a in ins[1:]],
        out_specs=[_rows(tm, D_MODEL), _full(st0.shape)],
        out_shape=[jax.ShapeDtypeStruct((n, D_MODEL), F32), st],
        scratch_shapes=[pltpu.VMEM(st0.shape, F32)],
        compiler_params=_params(1), name="ffn")(*ins)


def _trunk(x, st_conv, st_shift, st_wkv, st_re, st_im, st_ffn, pr, *, bq, t_len, tiles):
    n_ab, n_c = (DEPTH + 1) // 2, DEPTH // 2
    hist = CONV_A_WIDTH - 1
    c = min(WKV_CHUNK, 8 * ((t_len + 7) // 8))
    t_pad = c * ((t_len + c - 1) // c)
    o_conv, o_shift, o_wkv, o_re, o_im, o_ffn = [], [], [], [], [], []

    def to_seq(z):
        z = jnp.transpose(z.reshape(t_len, bq, D_B), (1, 0, 2))
        return jnp.pad(z, ((0, 0), (0, t_pad - t_len), (0, 0)))

    for i in range(DEPTH):
        j = i // 2
        if i % 2 == 0:
            conv0 = jnp.transpose(st_conv[:, j], (1, 0, 2)).reshape(hist * bq, D_A)
            outs = _ab_in(x, conv0, st_shift[:, j], pr["w_in_ab"][j], pr["conv_a_w"][j], pr["conv_a_b"][j],
                          pr["ln_a_g"][j], pr["ln_a_b"][j], pr["mu_b"][j], pr["w0_b"][j], pr["w2_pad"][j],
                          pr["a0_b"][j], pr["a2_pad"][j], pr["g2_b"][j], pr["k_k_b"][j], pr["k_a_b"][j],
                          pr["seg"], bq=bq, tm=tiles["ab_in"])
            ya, r, k, v, ld, kk, bv, g, conv_n, shift_n = outs
            yb_seq, wkv_n = _wkv(*(to_seq(z) for z in (r, k, v, ld, kk, bv, g)), st_wkv[:, j],
                                 pr["ln_x_g"][j], pr["ln_x_b"][j], pr["r_k_b"][j], c=c, sb=tiles["wkv_sb"])
            yb = jnp.transpose(yb_seq[:, :t_len], (1, 0, 2)).reshape(t_len * bq, D_B)
            x = _ab_out(x, ya, yb, pr["w_out_ab"][j], pr["ln1_g"][i], pr["ln1_b"][i], tm=tiles["ab_out"])
            o_conv.append(jnp.transpose(conv_n.reshape(hist, bq, D_A), (1, 0, 2)))
            o_shift.append(shift_n)
            o_wkv.append(wkv_n)
        else:
            x, h_re, h_im = _s5(x, st_re[:, j].reshape(bq, D_SSM), st_im[:, j].reshape(bq, D_SSM),
                                pr["w_in_c"][j], pr["bbd"][j], pr["cre"][j], pr["cim"][j], pr["a_re"][j],
                                pr["a_im"][j], pr["d_skip"][j], pr["w_out_c"][j], pr["ln1_g"][i],
                                pr["ln1_b"][i], bq=bq, tm=tiles["s5"], lb=tiles["s5_lb"])
            o_re.append(h_re.reshape(bq, G_C, P_C))
            o_im.append(h_im.reshape(bq, G_C, P_C))
        ffn0 = jnp.transpose(st_ffn[:, i], (1, 0, 2)).reshape((FFN_CONV_WIDTH - 1) * bq, D_FF)
        x, ffn_n = _ffn(x, ffn0, pr["w_up"][i], pr["conv_f_w"][i], pr["conv_f_b"][i], pr["w_down"][i],
                        pr["ln2_g"][i], pr["ln2_b"][i], bq=bq, tm=tiles["ffn"])
        o_ffn.append(jnp.transpose(ffn_n.reshape(FFN_CONV_WIDTH - 1, bq, D_FF), (1, 0, 2)))
    stack = lambda xs: jnp.stack(xs, 1)
    return x, stack(o_conv), stack(o_shift), stack(o_wkv), stack(o_re), stack(o_im), stack(o_ffn)


def _prepare(w_in_ab, conv_a_w, conv_a_b, ln_a_g, ln_a_b, mu_b, w0_b, w2_b, a0_b, a2_b, g2_b, k_k_b, k_a_b,
             r_k_b, ln_x_g, ln_x_b, w_out_ab, w_in_c, lam_re, lam_im, log_dt, b_re, b_im, c_re, c_im, d_skip,
             w_out_c, w_up, conv_f_w, conv_f_b, w_down, ln1_g, ln1_b, ln2_g, ln2_b):
    n_ab, n_c = w_in_ab.shape[0], w_in_c.shape[0]
    row = lambda a: a.reshape(a.shape[0], 1, -1)
    zeros = jnp.zeros((n_ab, LORA_W, D_B), F32)
    eye = jnp.eye(G_SLAB, dtype=F32)
    coefs = [_s5_coef(lam_re[j], lam_im[j], log_dt[j], b_re[j], b_im[j]) for j in range(n_c)]

    def drive(bb):
        bb = bb.reshape(C_GROUP, N_SLAB, G_SLAB, P_C)
        return jnp.einsum("csgp,gh->sgchp", bb, eye).reshape(N_SLAB, SLAB, SSM_SLAB)

    def readout(cc):
        cc = cc.reshape(N_SLAB, G_SLAB, C_GROUP, P_C)
        return jnp.einsum("sgcp,gh->shpgc", cc, eye).reshape(N_SLAB, SSM_SLAB, SLAB)

    return {
        "w_in_ab": w_in_ab.astype(BF16), "conv_a_w": conv_a_w, "conv_a_b": row(conv_a_b),
        "ln_a_g": row(ln_a_g), "ln_a_b": row(ln_a_b), "mu_b": row(mu_b), "w0_b": row(w0_b),
        "w2_pad": jnp.concatenate([w2_b, zeros], axis=1).astype(BF16), "a0_b": row(a0_b),
        "a2_pad": jnp.concatenate([zeros, a2_b], axis=1).astype(BF16), "g2_b": g2_b.astype(BF16),
        "k_k_b": row(k_k_b), "k_a_b": row(k_a_b), "r_k_b": row(r_k_b), "ln_x_g": row(ln_x_g),
        "ln_x_b": row(ln_x_b), "w_out_ab": w_out_ab.astype(BF16),
        "seg": jnp.kron(jnp.eye(H_B, dtype=F32), jnp.ones((HEAD_B, HEAD_B), F32)).astype(BF16),
        "w_in_c": w_in_c.astype(BF16),
        "bbd": jnp.stack([jnp.concatenate([drive(cf[2]), drive(cf[3])], axis=-1) for cf in coefs]).astype(BF16),
        "cre": jnp.stack([readout(c_re[j]) for j in range(n_c)]).astype(BF16),
        "cim": jnp.stack([readout(c_im[j]) for j in range(n_c)]).astype(BF16),
        "a_re": jnp.stack([cf[0].reshape(1, D_SSM) for cf in coefs]),
        "a_im": jnp.stack([cf[1].reshape(1, D_SSM) for cf in coefs]),
        "d_skip": row(d_skip), "w_out_c": w_out_c.astype(BF16),
        "w_up": w_up.astype(BF16), "conv_f_w": conv_f_w, "conv_f_b": row(conv_f_b),
        "w_down": w_down.astype(BF16),
        "ln1_g": row(ln1_g), "ln1_b": row(ln1_b), "ln2_g": row(ln2_g), "ln2_b": row(ln2_b),
    }


PROMPT_TILES = {"ab_in": 256, "wkv_sb": 2, "ab_out": 512, "s5": 256, "s5_lb": 1024, "ffn": 512}
SAMPLE_TILES = {"ab_in": 512, "wkv_sb": 8, "ab_out": 512, "s5": 512, "s5_lb": 128, "ffn": 512}


def _run_group(x, states, pr, tiles):
    bq, t_len, _ = x.shape
    xt = jnp.transpose(x, (1, 0, 2)).reshape(t_len * bq, D_MODEL)
    y, *new_states = _trunk(xt, *states, pr, bq=bq, t_len=t_len, tiles=tiles)
    return (jnp.transpose(y.reshape(t_len, bq, D_MODEL), (1, 0, 2)), *new_states)


def kernel(x_prompt, x_sample, state_conv_a, state_shift_b, state_wkv_b, state_ssm_re, state_ssm_im,
           state_conv_ffn, w_in_ab, conv_a_w, conv_a_b, ln_a_g, ln_a_b, mu_b, w0_b, w2_b, a0_b, a2_b, g2_b,
           k_k_b, k_a_b, r_k_b, ln_x_g, ln_x_b, w_out_ab, w_in_c, lam_re, lam_im, log_dt, b_re, b_im, c_re,
           c_im, d_skip, w_out_c, w_up, conv_f_w, conv_f_b, w_down, ln1_g, ln1_b, ln2_g, ln2_b):
    pr = _prepare(w_in_ab, conv_a_w, conv_a_b, ln_a_g, ln_a_b, mu_b, w0_b, w2_b, a0_b, a2_b, g2_b, k_k_b,
                  k_a_b, r_k_b.reshape(r_k_b.shape[0], D_B), ln_x_g, ln_x_b, w_out_ab, w_in_c, lam_re, lam_im,
                  log_dt, b_re, b_im, c_re, c_im, d_skip, w_out_c, w_up, conv_f_w, conv_f_b, w_down, ln1_g,
                  ln1_b, ln2_g, ln2_b)
    sample_states = (state_conv_a, state_shift_b, state_wkv_b, state_ssm_re, state_ssm_im, state_conv_ffn)
    bp = x_prompt.shape[0]
    prompt_states = tuple(jnp.zeros((bp,) + s.shape[1:], s.dtype) for s in sample_states)
    p = _run_group(x_prompt, prompt_states, pr, PROMPT_TILES)
    s = _run_group(x_sample, sample_states, pr, SAMPLE_TILES)
    return (p[0], s[0], p[1], s[1], p[2], s[2], p[3], s[3], p[4], s[4], p[5], s[5], p[6], s[6])
```

```python
import functools
import math

import jax
import jax.numpy as jnp
from jax import lax
from jax.experimental import pallas as pl
from jax.experimental.pallas import tpu as pltpu

F32 = jnp.float32
BF16 = jnp.bfloat16

D_MODEL = 1024
DEPTH = 4
D_A = 512
D_B = 512
HEAD_B = 64
H_B = D_B // HEAD_B
LORA_W = 64
LORA_A = 64
LORA_G = 128
D_BP = 3 * D_B + LORA_W + LORA_A + LORA_G
D_IN_AB = 2 * D_A + D_BP
CONV_A_WIDTH = 31
C_GROUP = 16
G_C = D_MODEL // C_GROUP
P_C = 64
D_SSM = G_C * P_C
SLAB = 128
N_SLAB = D_MODEL // SLAB
G_SLAB = SLAB // C_GROUP
SSM_SLAB = G_SLAB * P_C
D_FF = 2816
FFN_CONV_WIDTH = 3
ALPHA = (2 * DEPTH) ** 0.25
LN_EPS = 1e-5
GN_EPS = HEAD_B * 1e-5

VMEM_LIMIT_BYTES = 56 * 1024 * 1024
FFN_CHUNK = 256
WKV_CHUNK = 64
CONV_ROWS = 32


def _params(n_grid):
    return pltpu.CompilerParams(dimension_semantics=("arbitrary",) * n_grid,
                                vmem_limit_bytes=VMEM_LIMIT_BYTES)


def _full(shape):
    nd = len(shape)
    return pl.BlockSpec(shape, lambda *_: (0,) * nd)


def _rows(tm, width):
    return pl.BlockSpec((tm, width), lambda i: (i, 0))


def _dot(a, b):
    return jnp.dot(a.astype(BF16), b.astype(BF16), preferred_element_type=F32)


def _split(a):
    hi = a.astype(BF16)
    lo = (a - hi.astype(F32)).astype(BF16)
    return hi, lo


def _dg(a, b, ca, cb):
    return lax.dot_general(a, b, (((ca,), (cb,)), ((), ())), preferred_element_type=F32)


def _dot_hp(a, b, ca=1, cb=0):
    ah, al = _split(a)
    bh, bl = _split(b)
    return _dg(ah, bh, ca, cb) + (_dg(ah, bl, ca, cb) + _dg(al, bh, ca, cb))


def _layer_norm(x, g, b, eps):
    xc = x - jnp.mean(x, -1, keepdims=True)
    var = jnp.mean(xc * xc, -1, keepdims=True)
    return xc * lax.rsqrt(var + eps) * g + b


def _ab_in_kernel(x_ref, w_ref, cw_ref, cb_ref, lng_ref, lnb_ref, mu_ref, w0_ref, w2_ref, a0_ref,
                  a2_ref, g2_ref, kkw_ref, kaw_ref, seg_ref, conv0_ref, shift0_ref,
                  ya_ref, r_ref, k_ref, v_ref, ld_ref, kk_ref, b_ref, g_ref, convo_ref, shifto_ref,
                  ext_scr, sh_scr, *, bq, tm):
    hist = (CONV_A_WIDTH - 1) * bq

    @pl.when(pl.program_id(0) == 0)
    def _():
        ext_scr[0:hist, :] = conv0_ref[...]
        sh_scr[0:bq, :] = shift0_ref[...]

    p = _dot(x_ref[...], w_ref[...])

    ext_scr[hist:hist + tm, :] = p[:, :D_A] * jax.nn.sigmoid(p[:, D_A:2 * D_A])
    for r0 in range(0, tm, CONV_ROWS):
        acc = jnp.zeros((CONV_ROWS, D_A), F32) + cb_ref[...]
        for j in range(CONV_A_WIDTH):
            acc = acc + cw_ref[j:j + 1, :] * ext_scr[r0 + j * bq:r0 + j * bq + CONV_ROWS, :]
        ya = _layer_norm(acc, lng_ref[...], lnb_ref[...], LN_EPS)
        ya_ref[r0:r0 + CONV_ROWS, :] = ya * jax.nn.sigmoid(ya)
    new_hist = ext_scr[tm:tm + hist, :]
    convo_ref[...] = new_hist
    ext_scr[0:hist, :] = new_hist

    pb = p[:, 2 * D_A:]
    sh_scr[bq:bq + tm, :] = pb
    prev = sh_scr[0:tm, :]
    last = pb[tm - bq:, :]
    sh_scr[0:bq, :] = last
    shifto_ref[...] = last
    q = pb + (prev - pb) * mu_ref[...]
    r = q[:, :D_B]
    k = q[:, D_B:2 * D_B]
    v = q[:, 2 * D_B:3 * D_B]
    wa = q[:, 3 * D_B:3 * D_B + LORA_W + LORA_A]
    gl = q[:, 3 * D_B + LORA_W + LORA_A:]
    w = -jax.nn.softplus(-(w0_ref[...] + _dot(jnp.tanh(wa), w2_ref[...]))) - 0.5
    a = jax.nn.sigmoid(a0_ref[...] + _dot(wa, a2_ref[...]))
    kk = k * kkw_ref[...]
    sq_hi, sq_lo = _split(kk * kk)
    ssq = jnp.dot(sq_hi, seg_ref[...], preferred_element_type=F32) + jnp.dot(
        sq_lo, seg_ref[...], preferred_element_type=F32)
    kk = kk * lax.rsqrt(jnp.maximum(ssq, 1e-24))
    def put(ref, val):
        val = val.reshape(tm // bq, bq, D_B)
        pad = ref.shape[1] - tm // bq
        if pad:
            val = jnp.concatenate([val, jnp.zeros((pad, bq, D_B), F32)], axis=0)
        ref[...] = jnp.swapaxes(val, 0, 1)

    put(g_ref, _dot(jax.nn.sigmoid(gl), g2_ref[...]))
    put(r_ref, r)
    put(k_ref, k * (1.0 + (a - 1.0) * kaw_ref[...]))
    put(v_ref, v)
    put(ld_ref, -jnp.exp(w))
    put(kk_ref, kk)
    put(b_ref, kk * a)


def _ab_in(x, conv0, shift0, w_in, cw, cb, lng, lnb, mu, w0, w2p, a0, a2p, g2, kkw, kaw, seg, *, bq, tm, t_pad):
    n = x.shape[0]
    hist = (CONV_A_WIDTH - 1) * bq
    nt = n // tm
    steps = tm // bq
    assert n % tm == 0 and tm % bq == 0 and tm % CONV_ROWS == 0 and (nt == 1 or tm >= hist)
    t_blk = steps if nt > 1 else t_pad
    assert nt * t_blk == t_pad
    seq = jax.ShapeDtypeStruct((bq, t_pad, D_B), F32)
    seq_spec = pl.BlockSpec((bq, t_blk, D_B), lambda i: (0, i, 0))
    ins = [x, w_in, cw, cb, lng, lnb, mu, w0, w2p, a0, a2p, g2, kkw, kaw, seg, conv0, shift0]
    in_specs = [_rows(tm, D_MODEL)] + [_full(a.shape) for a in ins[1:]]
    return pl.pallas_call(
        functools.partial(_ab_in_kernel, bq=bq, tm=tm),
        grid=(nt,),
        in_specs=in_specs,
        out_specs=[_rows(tm, D_A)] + [seq_spec] * 7 + [_full((hist, D_A)), _full((bq, D_BP))],
        out_shape=[jax.ShapeDtypeStruct((n, D_A), F32)] + [seq] * 7
        + [jax.ShapeDtypeStruct((hist, D_A), F32), jax.ShapeDtypeStruct((bq, D_BP), F32)],
        scratch_shapes=[pltpu.VMEM((hist + tm, D_A), F32), pltpu.VMEM((bq + tm, D_BP), F32)],
        compiler_params=_params(1), name="ab_in")(*ins)


def _wkv_kernel(r_ref, k_ref, v_ref, ld_ref, kk_ref, b_ref, g_ref, s0_ref, lnxg_ref, lnxb_ref, rk_ref,
                y_ref, so_ref, s_scr, *, c, sb, n_sq):
    @pl.when(pl.program_id(1) == 0)
    def _():
        s_scr[...] = s0_ref[...]

    row = lax.broadcasted_iota(jnp.int32, (c, c), 0)
    col = lax.broadcasted_iota(jnp.int32, (c, c), 1)
    incl = row >= col
    strict = row > col
    tri = jnp.where(incl, 1.0, 0.0).astype(BF16)
    eye = jnp.where(row == col, 1.0, 0.0).astype(F32)

    left, right, v_hd, b_hd, k_hd, p_end, s_prev, tail = [], [], [], [], [], [], [], []
    for q in range(sb):
        ld = ld_ref[q]
        ld_hi = ld.astype(BF16)
        ld_rem = ld - ld_hi.astype(F32)
        ld_mid = ld_rem.astype(BF16)
        ld_lo = (ld_rem - ld_mid.astype(F32)).astype(BF16)
        cum = (jnp.dot(tri, ld_hi, preferred_element_type=F32)
               + (jnp.dot(tri, ld_mid, preferred_element_type=F32)
                  + jnp.dot(tri, ld_lo, preferred_element_type=F32)))
        p_in = jnp.exp(cum)
        p_inv = jnp.exp(-cum)
        kk_t = kk_ref[q] * jnp.exp(cum - ld)
        r_t = r_ref[q] * p_in
        b_t = b_ref[q] * p_inv
        k_t = k_ref[q] * p_inv
        r_all, k_all, v_all, g_all = r_ref[q], k_ref[q], v_ref[q], g_ref[q]
        for h in range(H_B):
            sl = slice(h * HEAD_B, (h + 1) * HEAD_B)
            left.append(jnp.concatenate([kk_t[:, sl], r_t[:, sl]], axis=0))
            right.append(jnp.concatenate([b_t[:, sl], k_t[:, sl]], axis=0))
            v_hd.append(v_all[:, sl])
            b_hd.append(b_t[:, sl])
            k_hd.append(k_t[:, sl])
            p_end.append(p_in[c - 1:c, sl])
            s_prev.append(s_scr[q, h])
            bonus = jnp.sum(r_all[:, sl] * k_all[:, sl] * rk_ref[:, sl], -1, keepdims=True) * v_all[:, sl]
            tail.append((q, sl, bonus, g_all[:, sl]))

    pairs = range(sb * H_B)
    left = [x.astype(BF16) for x in left]
    v_hd = [x.astype(BF16) for x in v_hd]
    gram = [_dg(left[i], right[i].astype(BF16), 1, 1) for i in pairs]
    a_k = [jnp.where(strict, gram[i][:c, c:], 0.0).astype(BF16) for i in pairs]
    a_rb = [jnp.where(incl, gram[i][c:, :c], 0.0).astype(BF16) for i in pairs]
    a_rk = [jnp.where(incl, gram[i][c:, c:], 0.0).astype(BF16) for i in pairs]
    n_pow = [jnp.where(strict, -gram[i][:c, :c], 0.0) for i in pairs]
    t_inv = [eye + n_pow[i] for i in pairs]
    gs = [_dg(left[i], s_prev[i].astype(BF16), 1, 1) for i in pairs]
    rhs = [gs[i][:c] + _dg(a_k[i], v_hd[i], 1, 0) for i in pairs]
    y_v = [gs[i][c:] + _dg(a_rk[i], v_hd[i], 1, 0) for i in pairs]
    s_v = [s_prev[i] + _dg(v_hd[i], k_hd[i].astype(BF16), 0, 0) for i in pairs]
    for _ in range(n_sq):
        n_pow = [_dot_hp(n_pow[i], n_pow[i]) for i in pairs]
        t_inv = [t_inv[i] + _dot_hp(t_inv[i], n_pow[i]) for i in pairs]
    u = [-_dot_hp(t_inv[i], rhs[i]) for i in pairs]
    y = [y_v[i] + _dg(a_rb[i], u[i].astype(BF16), 1, 0) for i in pairs]
    s_new = [(s_v[i] + _dg(u[i].astype(BF16), b_hd[i].astype(BF16), 0, 0)) * p_end[i] for i in pairs]
    for i in pairs:
        q, sl, bonus, g_h = tail[i]
        s_scr[q, i % H_B] = s_new[i]
        so_ref[q, i % H_B] = s_new[i]
        yc = y[i] - jnp.mean(y[i], -1, keepdims=True)
        yn = yc * lax.rsqrt(jnp.mean(yc * yc, -1, keepdims=True) + GN_EPS)
        y_ref[q, :, sl] = (yn * lnxg_ref[:, sl] + lnxb_ref[:, sl] + bonus) * g_h


def _wkv(r, k, v, ld, kk, b, g, s0, lnxg, lnxb, rk, *, c, sb):
    nb, t, _ = r.shape
    assert t % c == 0 and nb % sb == 0
    n_sq = max(int(math.ceil(math.log2(c))) - 1, 0)
    seq = pl.BlockSpec((sb, c, D_B), lambda i, j: (i, j, 0))
    st = pl.BlockSpec((sb, H_B, HEAD_B, HEAD_B), lambda i, j: (i, 0, 0, 0))
    vec = pl.BlockSpec((1, D_B), lambda i, j: (0, 0))
    return pl.pallas_call(
        functools.partial(_wkv_kernel, c=c, sb=sb, n_sq=n_sq),
        grid=(nb // sb, t // c),
        in_specs=[seq] * 7 + [st, vec, vec, vec],
        out_specs=[seq, st],
        out_shape=[jax.ShapeDtypeStruct((nb, t, D_B), F32),
                   jax.ShapeDtypeStruct((nb, H_B, HEAD_B, HEAD_B), F32)],
        scratch_shapes=[pltpu.VMEM((sb, H_B, HEAD_B, HEAD_B), F32)],
        compiler_params=_params(2), name="wkv")(r, k, v, ld, kk, b, g, s0, lnxg, lnxb, rk)


def _ab_out_kernel(x_ref, ya_ref, yb_ref, w_ref, g_ref, b_ref, o_ref, *, bq, tm):
    yb = jnp.swapaxes(yb_ref[...], 0, 1)[:tm // bq].reshape(tm, D_B)
    out = _dot(ya_ref[...], w_ref[0:D_A, :]) + _dot(yb, w_ref[D_A:, :])
    o_ref[...] = _layer_norm(ALPHA * x_ref[...] + out, g_ref[...], b_ref[...], LN_EPS)


def _ab_out(x, ya, yb, w_out, g, b, *, bq, tm):
    n = x.shape[0]
    nt = n // tm
    t_blk = tm // bq if nt > 1 else yb.shape[1]
    assert n % tm == 0 and nt * t_blk == yb.shape[1]
    return pl.pallas_call(
        functools.partial(_ab_out_kernel, bq=bq, tm=tm), grid=(nt,),
        in_specs=[_rows(tm, D_MODEL), _rows(tm, D_A), pl.BlockSpec((bq, t_blk, D_B), lambda i: (0, i, 0)),
                  _full(w_out.shape), _full(g.shape), _full(b.shape)],
        out_specs=_rows(tm, D_MODEL), out_shape=jax.ShapeDtypeStruct((n, D_MODEL), F32),
        compiler_params=_params(1), name="ab_out")(x, ya, yb, w_out, g, b)


def _s5_coef_kernel(lr_ref, li_ref, ldt_ref, br_ref, bi_ref, are_ref, aim_ref, bbr_ref, bbi_ref):
    lr, li = lr_ref[...], li_ref[...]
    dt = jnp.exp(ldt_ref[...])
    mag = jnp.exp(lr * dt)
    ab_re, ab_im = mag * jnp.cos(li * dt), mag * jnp.sin(li * dt)
    den = lr * lr + li * li
    nr, ni = ab_re - 1.0, ab_im
    f_re, f_im = (nr * lr + ni * li) / den, (ni * lr - nr * li) / den
    are_ref[...] = ab_re
    aim_ref[...] = ab_im
    for ch in range(C_GROUP):
        bbr_ref[ch] = f_re * br_ref[ch] - f_im * bi_ref[ch]
        bbi_ref[ch] = f_re * bi_ref[ch] + f_im * br_ref[ch]


def _s5_coef(lam_re, lam_im, log_dt, b_re, b_im):
    br = jnp.transpose(b_re, (2, 0, 1))
    bi = jnp.transpose(b_im, (2, 0, 1))
    gp = jax.ShapeDtypeStruct((G_C, P_C), F32)
    cgp = jax.ShapeDtypeStruct((C_GROUP, G_C, P_C), F32)
    return pl.pallas_call(_s5_coef_kernel, out_shape=[gp, gp, cgp, cgp], name="s5_coef")(
        lam_re, lam_im, log_dt[:, None], br, bi)


def _gelu_tanh(x):
    return 0.5 * x * (1.0 + jnp.tanh(math.sqrt(2.0 / math.pi) * (x + 0.044715 * (x * x * x))))


def _s5_kernel(x_ref, win_ref, bbd_ref, cre_ref, cim_ref, are_ref, aim_ref, dsk_ref, wout_ref, lng_ref,
               lnb_ref, hre0_ref, him0_ref, o_ref, hreo_ref, himo_ref, bre_scr, bim_scr, hre_scr, him_scr,
               *, bq, tm, lb):
    @pl.when(pl.program_id(0) == 0)
    def _():
        hre_scr[...] = hre0_ref[...]
        him_scr[...] = him0_ref[...]

    x = x_ref[...]
    u = _dot(x, win_ref[...])
    ub = u.astype(BF16)
    ys = []
    for j in range(N_SLAB):
        slab = slice(j * SSM_SLAB, (j + 1) * SSM_SLAB)
        bu = jnp.dot(ub[:, j * SLAB:(j + 1) * SLAB], bbd_ref[j], preferred_element_type=F32)
        bre_scr[:, slab] = bu[:, :SSM_SLAB]
        bim_scr[:, slab] = bu[:, SSM_SLAB:]
        for blk in range(SSM_SLAB // lb):
            cs = slice(j * SSM_SLAB + blk * lb, j * SSM_SLAB + (blk + 1) * lb)
            a_re, a_im = are_ref[:, cs], aim_ref[:, cs]
            h_re, h_im = hre_scr[:, cs], him_scr[:, cs]
            for t in range(tm // bq):
                rows = slice(t * bq, (t + 1) * bq)
                h_re, h_im = (a_re * h_re - a_im * h_im + bre_scr[rows, cs],
                              a_re * h_im + a_im * h_re + bim_scr[rows, cs])
                bre_scr[rows, cs] = h_re
                bim_scr[rows, cs] = h_im
            hre_scr[:, cs] = h_re
            him_scr[:, cs] = h_im
        ys.append(_dot(bre_scr[:, slab], cre_ref[j]) - _dot(bim_scr[:, slab], cim_ref[j]))
    hreo_ref[...] = hre_scr[...]
    himo_ref[...] = him_scr[...]
    y = jnp.concatenate(ys, axis=1) + dsk_ref[...] * u
    o = _dot(_gelu_tanh(y), wout_ref[...])
    out = o[:, :D_MODEL] * jax.nn.sigmoid(o[:, D_MODEL:])
    o_ref[...] = _layer_norm(ALPHA * x + out, lng_ref[...], lnb_ref[...], LN_EPS)


def _s5(x, h_re0, h_im0, w_in, bbd, cre, cim, a_re, a_im, dsk, w_out, lng, lnb, *, bq, tm, lb):
    n = x.shape[0]
    assert n % tm == 0 and tm % bq == 0
    ins = [x, w_in, bbd, cre, cim, a_re, a_im, dsk, w_out, lng, lnb, h_re0, h_im0]
    st = jax.ShapeDtypeStruct((bq, D_SSM), F32)
    return pl.pallas_call(
        functools.partial(_s5_kernel, bq=bq, tm=tm, lb=lb),
        grid=(n // tm,),
        in_specs=[_rows(tm, D_MODEL)] + [_full(a.shape) for a in ins[1:]],
        out_specs=[_rows(tm, D_MODEL), _full((bq, D_SSM)), _full((bq, D_SSM))],
        out_shape=[jax.ShapeDtypeStruct((n, D_MODEL), F32), st, st],
        scratch_shapes=[pltpu.VMEM((tm, D_SSM), F32), pltpu.VMEM((tm, D_SSM), F32),
                        pltpu.VMEM((bq, D_SSM), F32), pltpu.VMEM((bq, D_SSM), F32)],
        compiler_params=_params(1), name="s5")(*ins)


def _ffn_kernel(x_ref, wup_ref, cw_ref, cb_ref, wdn_ref, lng_ref, lnb_ref, st0_ref, o_ref, sto_ref,
                carry_scr, *, bq, tm):
    @pl.when(pl.program_id(0) == 0)
    def _():
        carry_scr[...] = st0_ref[...]

    x = x_ref[...]
    xb = x.astype(BF16)
    acc = jnp.zeros((tm, D_MODEL), F32)
    for ci in range(D_FF // FFN_CHUNK):
        cs = slice(ci * FFN_CHUNK, (ci + 1) * FFN_CHUNK)
        gs = slice(D_FF + ci * FFN_CHUNK, D_FF + (ci + 1) * FFN_CHUNK)
        h = jnp.dot(xb, wup_ref[:, cs], preferred_element_type=F32)
        gate = jnp.dot(xb, wup_ref[:, gs], preferred_element_type=F32)
        old = carry_scr[:, cs]
        h_m1 = jnp.concatenate([old[bq:], h[:tm - bq]], axis=0)
        h_m2 = jnp.concatenate([old, h[:tm - 2 * bq]], axis=0)
        cv = cw_ref[0:1, cs] * h_m2 + cw_ref[1:2, cs] * h_m1 + cw_ref[2:3, cs] * h + cb_ref[:, cs]
        act = cv * jax.nn.sigmoid(cv) * gate
        acc = acc + jnp.dot(act.astype(BF16), wdn_ref[cs, :], preferred_element_type=F32)
        carry_scr[:, cs] = h[tm - 2 * bq:]
    o_ref[...] = _layer_norm(ALPHA * x + acc, lng_ref[...], lnb_ref[...], LN_EPS)
    sto_ref[...] = carry_scr[...]


def _ffn(x, st0, w_up, cw, cb, w_dn, lng, lnb, *, bq, tm):
    n = x.shape[0]
    assert n % tm == 0 and tm >= 2 * bq and D_FF % FFN_CHUNK == 0
    ins = [x, w_up, cw, cb, w_dn, lng, lnb, st0]
    st = jax.ShapeDtypeStruct(st0.shape, F32)
    return pl.pallas_call(
        functools.partial(_ffn_kernel, bq=bq, tm=tm),
        grid=(n // tm,),
        in_specs=[_rows(tm, D_MODEL)] + [_full(a.shape) for a in ins[1:]],
        out_specs=[_rows(tm, D_MODEL), _full(st0.shape)],
        out_shape=[jax.ShapeDtypeStruct((n, D_MODEL), F32), st],
        scratch_shapes=[pltpu.VMEM(st0.shape, F32)],
        compiler_params=_params(1), name="ffn")(*ins)


def _trunk(x, st_conv, st_shift, st_wkv, st_re, st_im, st_ffn, pr, *, bq, t_len, tiles):
    n_ab, n_c = (DEPTH + 1) // 2, DEPTH // 2
    hist = CONV_A_WIDTH - 1
    c = min(WKV_CHUNK, 8 * ((t_len + 7) // 8))
    t_pad = c * ((t_len + c - 1) // c)
    o_conv, o_shift, o_wkv, o_re, o_im, o_ffn = [], [], [], [], [], []

    for i in range(DEPTH):
        j = i // 2
        if i % 2 == 0:
            conv0 = jnp.transpose(st_conv[:, j], (1, 0, 2)).reshape(hist * bq, D_A)
            outs = _ab_in(x, conv0, st_shift[:, j], pr["w_in_ab"][j], pr["conv_a_w"][j], pr["conv_a_b"][j],
                          pr["ln_a_g"][j], pr["ln_a_b"][j], pr["mu_b"][j], pr["w0_b"][j], pr["w2_pad"][j],
                          pr["a0_b"][j], pr["a2_pad"][j], pr["g2_b"][j], pr["k_k_b"][j], pr["k_a_b"][j],
                          pr["seg"], bq=bq, tm=tiles["ab_in"], t_pad=t_pad)
            ya, r, k, v, ld, kk, bv, g, conv_n, shift_n = outs
            yb, wkv_n = _wkv(r, k, v, ld, kk, bv, g, st_wkv[:, j], pr["ln_x_g"][j], pr["ln_x_b"][j],
                             pr["r_k_b"][j], c=c, sb=tiles["wkv_sb"])
            x = _ab_out(x, ya, yb, pr["w_out_ab"][j], pr["ln1_g"][i], pr["ln1_b"][i], bq=bq, tm=tiles["ab_out"])
            o_conv.append(jnp.transpose(conv_n.reshape(hist, bq, D_A), (1, 0, 2)))
            o_shift.append(shift_n)
            o_wkv.append(wkv_n)
        else:
            x, h_re, h_im = _s5(x, st_re[:, j].reshape(bq, D_SSM), st_im[:, j].reshape(bq, D_SSM),
                                pr["w_in_c"][j], pr["bbd"][j], pr["cre"][j], pr["cim"][j], pr["a_re"][j],
                                pr["a_im"][j], pr["d_skip"][j], pr["w_out_c"][j], pr["ln1_g"][i],
                                pr["ln1_b"][i], bq=bq, tm=tiles["s5"], lb=tiles["s5_lb"])
            o_re.append(h_re.reshape(bq, G_C, P_C))
            o_im.append(h_im.reshape(bq, G_C, P_C))
        ffn0 = jnp.transpose(st_ffn[:, i], (1, 0, 2)).reshape((FFN_CONV_WIDTH - 1) * bq, D_FF)
        x, ffn_n = _ffn(x, ffn0, pr["w_up"][i], pr["conv_f_w"][i], pr["conv_f_b"][i], pr["w_down"][i],
                        pr["ln2_g"][i], pr["ln2_b"][i], bq=bq, tm=tiles["ffn"])
        o_ffn.append(jnp.transpose(ffn_n.reshape(FFN_CONV_WIDTH - 1, bq, D_FF), (1, 0, 2)))
    stack = lambda xs: jnp.stack(xs, 1)
    return x, stack(o_conv), stack(o_shift), stack(o_wkv), stack(o_re), stack(o_im), stack(o_ffn)


def _prepare(w_in_ab, conv_a_w, conv_a_b, ln_a_g, ln_a_b, mu_b, w0_b, w2_b, a0_b, a2_b, g2_b, k_k_b, k_a_b,
             r_k_b, ln_x_g, ln_x_b, w_out_ab, w_in_c, lam_re, lam_im, log_dt, b_re, b_im, c_re, c_im, d_skip,
             w_out_c, w_up, conv_f_w, conv_f_b, w_down, ln1_g, ln1_b, ln2_g, ln2_b):
    n_ab, n_c = w_in_ab.shape[0], w_in_c.shape[0]
    row = lambda a: a.reshape(a.shape[0], 1, -1)
    zeros = jnp.zeros((n_ab, LORA_W, D_B), F32)
    eye = jnp.eye(G_SLAB, dtype=F32)
    coefs = [_s5_coef(lam_re[j], lam_im[j], log_dt[j], b_re[j], b_im[j]) for j in range(n_c)]

    def drive(bb):
        bb = bb.reshape(C_GROUP, N_SLAB, G_SLAB, P_C)
        return jnp.einsum("csgp,gh->sgchp", bb, eye).reshape(N_SLAB, SLAB, SSM_SLAB)

    def readout(cc):
        cc = cc.reshape(N_SLAB, G_SLAB, C_GROUP, P_C)
        return jnp.einsum("sgcp,gh->shpgc", cc, eye).reshape(N_SLAB, SSM_SLAB, SLAB)

    return {
        "w_in_ab": w_in_ab.astype(BF16), "conv_a_w": conv_a_w, "conv_a_b": row(conv_a_b),
        "ln_a_g": row(ln_a_g), "ln_a_b": row(ln_a_b), "mu_b": row(mu_b), "w0_b": row(w0_b),
        "w2_pad": jnp.concatenate([w2_b, zeros], axis=1).astype(BF16), "a0_b": row(a0_b),
        "a2_pad": jnp.concatenate([zeros, a2_b], axis=1).astype(BF16), "g2_b": g2_b.astype(BF16),
        "k_k_b": row(k_k_b), "k_a_b": row(k_a_b), "r_k_b": row(r_k_b), "ln_x_g": row(ln_x_g),
        "ln_x_b": row(ln_x_b), "w_out_ab": w_out_ab.astype(BF16),
        "seg": jnp.kron(jnp.eye(H_B, dtype=F32), jnp.ones((HEAD_B, HEAD_B), F32)).astype(BF16),
        "w_in_c": w_in_c.astype(BF16),
        "bbd": jnp.stack([jnp.concatenate([drive(cf[2]), drive(cf[3])], axis=-1) for cf in coefs]).astype(BF16),
        "cre": jnp.stack([readout(c_re[j]) for j in range(n_c)]).astype(BF16),
        "cim": jnp.stack([readout(c_im[j]) for j in range(n_c)]).astype(BF16),
        "a_re": jnp.stack([cf[0].reshape(1, D_SSM) for cf in coefs]),
        "a_im": jnp.stack([cf[1].reshape(1, D_SSM) for cf in coefs]),
        "d_skip": row(d_skip), "w_out_c": w_out_c.astype(BF16),
        "w_up": w_up.astype(BF16), "conv_f_w": conv_f_w, "conv_f_b": row(conv_f_b),
        "w_down": w_down.astype(BF16),
        "ln1_g": row(ln1_g), "ln1_b": row(ln1_b), "ln2_g": row(ln2_g), "ln2_b": row(ln2_b),
    }


PROMPT_TILES = {"ab_in": 256, "wkv_sb": 2, "ab_out": 512, "s5": 256, "s5_lb": 512, "ffn": 512}
SAMPLE_TILES = {"ab_in": 512, "wkv_sb": 8, "ab_out": 512, "s5": 512, "s5_lb": 128, "ffn": 512}


def _run_group(x, states, pr, tiles):
    bq, t_len, _ = x.shape
    xt = jnp.transpose(x, (1, 0, 2)).reshape(t_len * bq, D_MODEL)
    y, *new_states = _trunk(xt, *states, pr, bq=bq, t_len=t_len, tiles=tiles)
    return (jnp.transpose(y.reshape(t_len, bq, D_MODEL), (1, 0, 2)), *new_states)


def kernel(x_prompt, x_sample, state_conv_a, state_shift_b, state_wkv_b, state_ssm_re, state_ssm_im,
           state_conv_ffn, w_in_ab, conv_a_w, conv_a_b, ln_a_g, ln_a_b, mu_b, w0_b, w2_b, a0_b, a2_b, g2_b,
           k_k_b, k_a_b, r_k_b, ln_x_g, ln_x_b, w_out_ab, w_in_c, lam_re, lam_im, log_dt, b_re, b_im, c_re,
           c_im, d_skip, w_out_c, w_up, conv_f_w, conv_f_b, w_down, ln1_g, ln1_b, ln2_g, ln2_b):
    pr = _prepare(w_in_ab, conv_a_w, conv_a_b, ln_a_g, ln_a_b, mu_b, w0_b, w2_b, a0_b, a2_b, g2_b, k_k_b,
                  k_a_b, r_k_b.reshape(r_k_b.shape[0], D_B), ln_x_g, ln_x_b, w_out_ab, w_in_c, lam_re, lam_im,
                  log_dt, b_re, b_im, c_re, c_im, d_skip, w_out_c, w_up, conv_f_w, conv_f_b, w_down, ln1_g,
                  ln1_b, ln2_g, ln2_b)
    sample_states = (state_conv_a, state_shift_b, state_wkv_b, state_ssm_re, state_ssm_im, state_conv_ffn)
    bp = x_prompt.shape[0]
    prompt_states = tuple(jnp.zeros((bp,) + s.shape[1:], s.dtype) for s in sample_states)
    p = _run_group(x_prompt, prompt_states, pr, PROMPT_TILES)
    s = _run_group(x_sample, sample_states, pr, SAMPLE_TILES)
    return (p[0], s[0], p[1], s[1], p[2], s[2], p[3], s[3], p[4], s[4], p[5], s[5], p[6], s[6])
```

```python
import functools
import math

import jax
import jax.numpy as jnp
from jax import lax
from jax.experimental import pallas as pl
from jax.experimental.pallas import tpu as pltpu

F32 = jnp.float32
BF16 = jnp.bfloat16

D_MODEL = 1024
DEPTH = 4
D_A = 512
D_B = 512
HEAD_B = 64
H_B = D_B // HEAD_B
LORA_W = 64
LORA_A = 64
LORA_G = 128
D_BP = 3 * D_B + LORA_W + LORA_A + LORA_G
D_IN_AB = 2 * D_A + D_BP
CONV_A_WIDTH = 31
C_GROUP = 16
G_C = D_MODEL // C_GROUP
P_C = 64
D_SSM = G_C * P_C
SLAB = 128
N_SLAB = D_MODEL // SLAB
G_SLAB = SLAB // C_GROUP
SSM_SLAB = G_SLAB * P_C
D_FF = 2816
FFN_CONV_WIDTH = 3
ALPHA = (2 * DEPTH) ** 0.25
LN_EPS = 1e-5
GN_EPS = HEAD_B * 1e-5

VMEM_LIMIT_BYTES = 56 * 1024 * 1024
FFN_CHUNK = 256
WKV_CHUNK = 64
WKV_GROUP = 4
WKV_LANES = WKV_GROUP * HEAD_B
CONV_ROWS = 32


def _params(n_grid):
    return pltpu.CompilerParams(dimension_semantics=("arbitrary",) * n_grid,
                                vmem_limit_bytes=VMEM_LIMIT_BYTES)


def _full(shape):
    nd = len(shape)
    return pl.BlockSpec(shape, lambda *_: (0,) * nd)


def _rows(tm, width):
    return pl.BlockSpec((tm, width), lambda i: (i, 0))


def _dot(a, b):
    return jnp.dot(a.astype(BF16), b.astype(BF16), preferred_element_type=F32)


def _split(a):
    hi = a.astype(BF16)
    lo = (a - hi.astype(F32)).astype(BF16)
    return hi, lo


def _dg(a, b, ca, cb):
    return lax.dot_general(a, b, (((ca,), (cb,)), ((), ())), preferred_element_type=F32)


def _dot_hp(a, b, ca=1, cb=0):
    ah, al = _split(a)
    bh, bl = _split(b)
    return _dg(ah, bh, ca, cb) + (_dg(ah, bl, ca, cb) + _dg(al, bh, ca, cb))


def _layer_norm(x, g, b, eps):
    xc = x - jnp.mean(x, -1, keepdims=True)
    var = jnp.mean(xc * xc, -1, keepdims=True)
    return xc * lax.rsqrt(var + eps) * g + b


def _ab_in_kernel(x_ref, w_ref, cw_ref, cb_ref, lng_ref, lnb_ref, mu_ref, w0_ref, w2_ref, a0_ref,
                  a2_ref, g2_ref, kkw_ref, kaw_ref, seg_ref, conv0_ref, shift0_ref,
                  ya_ref, r_ref, k_ref, v_ref, ld_ref, kk_ref, b_ref, g_ref, convo_ref, shifto_ref,
                  ext_scr, sh_scr, *, bq, tm):
    hist = (CONV_A_WIDTH - 1) * bq

    @pl.when(pl.program_id(0) == 0)
    def _():
        ext_scr[0:hist, :] = conv0_ref[...]
        sh_scr[0:bq, :] = shift0_ref[...]

    p = _dot(x_ref[...], w_ref[...])

    ext_scr[hist:hist + tm, :] = p[:, :D_A] * jax.nn.sigmoid(p[:, D_A:2 * D_A])
    for r0 in range(0, tm, CONV_ROWS):
        acc = jnp.zeros((CONV_ROWS, D_A), F32) + cb_ref[...]
        for j in range(CONV_A_WIDTH):
            acc = acc + cw_ref[j:j + 1, :] * ext_scr[r0 + j * bq:r0 + j * bq + CONV_ROWS, :]
        ya = _layer_norm(acc, lng_ref[...], lnb_ref[...], LN_EPS)
        ya_ref[r0:r0 + CONV_ROWS, :] = ya * jax.nn.sigmoid(ya)
    new_hist = ext_scr[tm:tm + hist, :]
    convo_ref[...] = new_hist
    ext_scr[0:hist, :] = new_hist

    pb = p[:, 2 * D_A:]
    sh_scr[bq:bq + tm, :] = pb
    prev = sh_scr[0:tm, :]
    last = pb[tm - bq:, :]
    sh_scr[0:bq, :] = last
    shifto_ref[...] = last
    q = pb + (prev - pb) * mu_ref[...]
    r = q[:, :D_B]
    k = q[:, D_B:2 * D_B]
    v = q[:, 2 * D_B:3 * D_B]
    wa = q[:, 3 * D_B:3 * D_B + LORA_W + LORA_A]
    gl = q[:, 3 * D_B + LORA_W + LORA_A:]
    w = -jax.nn.softplus(-(w0_ref[...] + _dot(jnp.tanh(wa), w2_ref[...]))) - 0.5
    a = jax.nn.sigmoid(a0_ref[...] + _dot(wa, a2_ref[...]))
    kk = k * kkw_ref[...]
    sq_hi, sq_lo = _split(kk * kk)
    ssq = jnp.dot(sq_hi, seg_ref[...], preferred_element_type=F32) + jnp.dot(
        sq_lo, seg_ref[...], preferred_element_type=F32)
    kk = kk * lax.rsqrt(jnp.maximum(ssq, 1e-24))
    def put(ref, val):
        val = val.reshape(tm // bq, bq, D_B)
        pad = ref.shape[1] - tm // bq
        if pad:
            val = jnp.concatenate([val, jnp.zeros((pad, bq, D_B), F32)], axis=0)
        ref[...] = jnp.swapaxes(val, 0, 1)

    put(g_ref, _dot(jax.nn.sigmoid(gl), g2_ref[...]))
    put(r_ref, r)
    put(k_ref, k * (1.0 + (a - 1.0) * kaw_ref[...]))
    put(v_ref, v)
    put(ld_ref, -jnp.exp(w))
    put(kk_ref, kk)
    put(b_ref, kk * a)


def _ab_in(x, conv0, shift0, w_in, cw, cb, lng, lnb, mu, w0, w2p, a0, a2p, g2, kkw, kaw, seg, *, bq, tm, t_pad):
    n = x.shape[0]
    hist = (CONV_A_WIDTH - 1) * bq
    nt = n // tm
    steps = tm // bq
    assert n % tm == 0 and tm % bq == 0 and tm % CONV_ROWS == 0 and (nt == 1 or tm >= hist)
    t_blk = steps if nt > 1 else t_pad
    assert nt * t_blk == t_pad
    seq = jax.ShapeDtypeStruct((bq, t_pad, D_B), F32)
    seq_spec = pl.BlockSpec((bq, t_blk, D_B), lambda i: (0, i, 0))
    ins = [x, w_in, cw, cb, lng, lnb, mu, w0, w2p, a0, a2p, g2, kkw, kaw, seg, conv0, shift0]
    in_specs = [_rows(tm, D_MODEL)] + [_full(a.shape) for a in ins[1:]]
    return pl.pallas_call(
        functools.partial(_ab_in_kernel, bq=bq, tm=tm),
        grid=(nt,),
        in_specs=in_specs,
        out_specs=[_rows(tm, D_A)] + [seq_spec] * 7 + [_full((hist, D_A)), _full((bq, D_BP))],
        out_shape=[jax.ShapeDtypeStruct((n, D_A), F32)] + [seq] * 7
        + [jax.ShapeDtypeStruct((hist, D_A), F32), jax.ShapeDtypeStruct((bq, D_BP), F32)],
        scratch_shapes=[pltpu.VMEM((hist + tm, D_A), F32), pltpu.VMEM((bq + tm, D_BP), F32)],
        compiler_params=_params(1), name="ab_in")(*ins)


def _head_blocks(y, width):
    lane_head = lax.broadcasted_iota(jnp.int32, y.shape, 1) // width
    return jnp.concatenate([jnp.where(lane_head == h, y, jnp.zeros_like(y)) for h in range(WKV_GROUP)], axis=0)


def _per_head(x, y, width):
    return _dg(x.astype(BF16), _head_blocks(y.astype(BF16), width), 1, 0)


def _per_head_hp(x, y, width):
    xh, xl = _split(x)
    yh, yl = _split(y)
    m = x.shape[0]
    both = _dg(jnp.concatenate([xh, xl], axis=0), _head_blocks(yh, width), 1, 0)
    return both[:m] + (both[m:] + _dg(xh, _head_blocks(yl, width), 1, 0))


def _wkv_kernel(r_ref, k_ref, v_ref, ld_ref, kk_ref, b_ref, g_ref, s0_ref, lnxg_ref, lnxb_ref, rk_ref, seg_ref,
                y_ref, so_ref, s_scr, *, c, sb, n_sq):
    n_grp = H_B // WKV_GROUP
    gc = WKV_GROUP * c
    blk_r = lax.broadcasted_iota(jnp.int32, (WKV_LANES, WKV_LANES), 0) // HEAD_B
    blk_c = lax.broadcasted_iota(jnp.int32, (WKV_LANES, WKV_LANES), 1) // HEAD_B
    same_head = blk_r == blk_c

    @pl.when(pl.program_id(1) == 0)
    def _():
        for q in range(sb):
            for gi in range(n_grp):
                s = s0_ref[q, gi * WKV_GROUP:(gi + 1) * WKV_GROUP].reshape(WKV_LANES, HEAD_B)
                s_scr[q, gi] = jnp.where(same_head, jnp.concatenate([s] * WKV_GROUP, axis=1), 0.0)

    row = lax.broadcasted_iota(jnp.int32, (c, c), 0)
    col = lax.broadcasted_iota(jnp.int32, (c, c), 1)
    tri = jnp.where(row >= col, 1.0, 0.0).astype(BF16)
    row_g = lax.broadcasted_iota(jnp.int32, (c, gc), 0)
    col_g = lax.broadcasted_iota(jnp.int32, (c, gc), 1) % c
    incl = row_g >= col_g
    strict = row_g > col_g
    eye = jnp.where(row_g == col_g, 1.0, 0.0).astype(F32)

    left, right, v_g, bk_g, p_end, s_prev = [], [], [], [], [], []
    for q in range(sb):
        ld = ld_ref[q]
        ld_hi = ld.astype(BF16)
        ld_rem = ld - ld_hi.astype(F32)
        ld_mid = ld_rem.astype(BF16)
        ld_lo = (ld_rem - ld_mid.astype(F32)).astype(BF16)
        cum = (jnp.dot(tri, ld_hi, preferred_element_type=F32)
               + (jnp.dot(tri, ld_mid, preferred_element_type=F32)
                  + jnp.dot(tri, ld_lo, preferred_element_type=F32)))
        p_in = jnp.exp(cum)
        p_inv = jnp.exp(-cum)
        kk_t = kk_ref[q] * jnp.exp(cum - ld)
        r_t = r_ref[q] * p_in
        b_t = b_ref[q] * p_inv
        k_t = k_ref[q] * p_inv
        v_all = v_ref[q]
        for gi in range(n_grp):
            gl = slice(gi * WKV_LANES, (gi + 1) * WKV_LANES)
            left.append(jnp.concatenate([kk_t[:, gl], r_t[:, gl]], axis=0).astype(BF16))
            b_bf, k_bf = b_t[:, gl].astype(BF16), k_t[:, gl].astype(BF16)
            right.append(jnp.concatenate([_head_blocks(b_bf, HEAD_B), _head_blocks(k_bf, HEAD_B)], axis=0))
            bk_g.append(jnp.concatenate([b_bf, k_bf], axis=0))
            v_g.append(v_all[:, gl])
            p_end.append(p_in[c - 1:c, gl])
            s_prev.append(s_scr[q, gi])

    chains = range(sb * n_grp)
    gram = [_dg(left[i], right[i], 1, 1) for i in chains]
    a_b = [jnp.where(strict, gram[i][:c, :gc], 0.0) for i in chains]
    a_k = [jnp.where(strict, gram[i][:c, gc:], 0.0) for i in chains]
    a_rb = [jnp.where(incl, gram[i][c:, :gc], 0.0) for i in chains]
    a_rk = [jnp.where(incl, gram[i][c:, gc:], 0.0) for i in chains]
    gs = [_dg(left[i], s_prev[i].astype(BF16), 1, 1) for i in chains]
    av = [_per_head(jnp.concatenate([a_k[i], a_rk[i]], axis=0), v_g[i], HEAD_B) for i in chains]
    rhs = [gs[i][:c] + av[i][:c] for i in chains]
    y_v = [gs[i][c:] + av[i][c:] for i in chains]
    n_pow = [-a_b[i] for i in chains]
    t0 = [eye + n_pow[i] for i in chains]
    if n_sq:
        n_pow = [_per_head(n_pow[i], n_pow[i], c) for i in chains]
        for _ in range(n_sq - 1):
            both = [_per_head(jnp.concatenate([t0[i], n_pow[i]], axis=0), n_pow[i], c) for i in chains]
            t0 = [t0[i] + both[i][:c] for i in chains]
            n_pow = [both[i][c:] for i in chains]
        t0 = [t0[i] + _per_head(t0[i], n_pow[i], c) for i in chains]
    res = [eye - t0[i] - _per_head_hp(a_b[i], t0[i], c) for i in chains]
    z = [rhs[i] + _per_head(res[i], rhs[i], HEAD_B) for i in chains]
    u = [-_per_head_hp(t0[i], z[i], HEAD_B) for i in chains]
    y = [y_v[i] + _per_head(a_rb[i], u[i], HEAD_B) for i in chains]
    upd = [_dg(jnp.concatenate([u[i], v_g[i]], axis=0).astype(BF16), bk_g[i], 0, 0) for i in chains]
    s_new = [(s_prev[i] + jnp.where(same_head, upd[i], 0.0)) * p_end[i] for i in chains]
    for i in chains:
        s_scr[i // n_grp, i % n_grp] = s_new[i]

    rows = sb * c

    def head_sum(x):
        hi, lo = _split(x)
        parts = [hi[:, :WKV_LANES], lo[:, :WKV_LANES], hi[:, WKV_LANES:], lo[:, WKV_LANES:]]
        s = jnp.dot(jnp.concatenate(parts, axis=0), seg_ref[...], preferred_element_type=F32)
        return jnp.concatenate([s[:rows] + s[rows:2 * rows], s[2 * rows:3 * rows] + s[3 * rows:]], axis=1)

    flat = lambda ref: ref[...].reshape(rows, D_B)
    y_all = jnp.concatenate([jnp.concatenate(y[q * n_grp:(q + 1) * n_grp], axis=1) for q in range(sb)], axis=0)
    yc = y_all - head_sum(y_all) * (1.0 / HEAD_B)
    yn = yc * lax.rsqrt(head_sum(yc * yc) * (1.0 / HEAD_B) + GN_EPS)
    bonus = head_sum(flat(r_ref) * flat(k_ref) * rk_ref[...]) * flat(v_ref)
    y_ref[...] = ((yn * lnxg_ref[...] + lnxb_ref[...] + bonus) * flat(g_ref)).reshape(sb, c, D_B)

    @pl.when(pl.program_id(1) == pl.num_programs(1) - 1)
    def _():
        for q in range(sb):
            for gi in range(n_grp):
                for h in range(WKV_GROUP):
                    hl = slice(h * HEAD_B, (h + 1) * HEAD_B)
                    so_ref[q, gi * WKV_GROUP + h] = s_scr[q, gi, hl, hl]


def _wkv(r, k, v, ld, kk, b, g, s0, lnxg, lnxb, rk, seg, *, c, sb):
    nb, t, _ = r.shape
    assert t % c == 0 and nb % sb == 0
    n_sq = max(int(math.ceil(math.log2(c))) - 1, 0)
    seq = pl.BlockSpec((sb, c, D_B), lambda i, j: (i, j, 0))
    st = pl.BlockSpec((sb, H_B, HEAD_B, HEAD_B), lambda i, j: (i, 0, 0, 0))
    vec = pl.BlockSpec((1, D_B), lambda i, j: (0, 0))
    return pl.pallas_call(
        functools.partial(_wkv_kernel, c=c, sb=sb, n_sq=n_sq),
        grid=(nb // sb, t // c),
        in_specs=[seq] * 7 + [st, vec, vec, vec, pl.BlockSpec((WKV_LANES, WKV_LANES), lambda i, j: (0, 0))],
        out_specs=[seq, st],
        out_shape=[jax.ShapeDtypeStruct((nb, t, D_B), F32),
                   jax.ShapeDtypeStruct((nb, H_B, HEAD_B, HEAD_B), F32)],
        scratch_shapes=[pltpu.VMEM((sb, H_B // WKV_GROUP, WKV_LANES, WKV_LANES), F32)],
        compiler_params=_params(2), name="wkv")(r, k, v, ld, kk, b, g, s0, lnxg, lnxb, rk, seg)


def _ab_out_kernel(x_ref, ya_ref, yb_ref, w_ref, g_ref, b_ref, o_ref, *, bq, tm):
    yb = jnp.swapaxes(yb_ref[...], 0, 1)[:tm // bq].reshape(tm, D_B)
    out = _dot(ya_ref[...], w_ref[0:D_A, :]) + _dot(yb, w_ref[D_A:, :])
    o_ref[...] = _layer_norm(ALPHA * x_ref[...] + out, g_ref[...], b_ref[...], LN_EPS)


def _ab_out(x, ya, yb, w_out, g, b, *, bq, tm):
    n = x.shape[0]
    nt = n // tm
    t_blk = tm // bq if nt > 1 else yb.shape[1]
    assert n % tm == 0 and nt * t_blk == yb.shape[1]
    return pl.pallas_call(
        functools.partial(_ab_out_kernel, bq=bq, tm=tm), grid=(nt,),
        in_specs=[_rows(tm, D_MODEL), _rows(tm, D_A), pl.BlockSpec((bq, t_blk, D_B), lambda i: (0, i, 0)),
                  _full(w_out.shape), _full(g.shape), _full(b.shape)],
        out_specs=_rows(tm, D_MODEL), out_shape=jax.ShapeDtypeStruct((n, D_MODEL), F32),
        compiler_params=_params(1), name="ab_out")(x, ya, yb, w_out, g, b)


def _s5_coef_kernel(lr_ref, li_ref, ldt_ref, br_ref, bi_ref, are_ref, aim_ref, bbr_ref, bbi_ref):
    lr, li = lr_ref[...], li_ref[...]
    dt = jnp.exp(ldt_ref[...])
    mag = jnp.exp(lr * dt)
    ab_re, ab_im = mag * jnp.cos(li * dt), mag * jnp.sin(li * dt)
    den = lr * lr + li * li
    nr, ni = ab_re - 1.0, ab_im
    f_re, f_im = (nr * lr + ni * li) / den, (ni * lr - nr * li) / den
    are_ref[...] = ab_re
    aim_ref[...] = ab_im
    for ch in range(C_GROUP):
        bbr_ref[ch] = f_re * br_ref[ch] - f_im * bi_ref[ch]
        bbi_ref[ch] = f_re * bi_ref[ch] + f_im * br_ref[ch]


def _s5_coef(lam_re, lam_im, log_dt, b_re, b_im):
    br = jnp.transpose(b_re, (2, 0, 1))
    bi = jnp.transpose(b_im, (2, 0, 1))
    gp = jax.ShapeDtypeStruct((G_C, P_C), F32)
    cgp = jax.ShapeDtypeStruct((C_GROUP, G_C, P_C), F32)
    return pl.pallas_call(_s5_coef_kernel, out_shape=[gp, gp, cgp, cgp], name="s5_coef")(
        lam_re, lam_im, log_dt[:, None], br, bi)


def _gelu_tanh(x):
    return 0.5 * x * (1.0 + jnp.tanh(math.sqrt(2.0 / math.pi) * (x + 0.044715 * (x * x * x))))


def _s5_kernel(x_ref, win_ref, bbd_ref, cre_ref, cim_ref, are_ref, aim_ref, dsk_ref, wout_ref, lng_ref,
               lnb_ref, hre0_ref, him0_ref, o_ref, hreo_ref, himo_ref, bre_scr, bim_scr, hre_scr, him_scr,
               *, bq, tm, lb):
    @pl.when(pl.program_id(0) == 0)
    def _():
        hre_scr[...] = hre0_ref[...]
        him_scr[...] = him0_ref[...]

    x = x_ref[...]
    u = _dot(x, win_ref[...])
    ub = u.astype(BF16)
    ys = []
    for j in range(N_SLAB):
        slab = slice(j * SSM_SLAB, (j + 1) * SSM_SLAB)
        bu = jnp.dot(ub[:, j * SLAB:(j + 1) * SLAB], bbd_ref[j], preferred_element_type=F32)
        bre_scr[:, slab] = bu[:, :SSM_SLAB]
        bim_scr[:, slab] = bu[:, SSM_SLAB:]
        for blk in range(SSM_SLAB // lb):
            cs = slice(j * SSM_SLAB + blk * lb, j * SSM_SLAB + (blk + 1) * lb)
            a_re, a_im = are_ref[:, cs], aim_ref[:, cs]
            h_re, h_im = hre_scr[:, cs], him_scr[:, cs]
            for t in range(tm // bq):
                rows = slice(t * bq, (t + 1) * bq)
                h_re, h_im = (a_re * h_re - a_im * h_im + bre_scr[rows, cs],
                              a_re * h_im + a_im * h_re + bim_scr[rows, cs])
                bre_scr[rows, cs] = h_re
                bim_scr[rows, cs] = h_im
            hre_scr[:, cs] = h_re
            him_scr[:, cs] = h_im
        ys.append(_dot(bre_scr[:, slab], cre_ref[j]) - _dot(bim_scr[:, slab], cim_ref[j]))
    hreo_ref[...] = hre_scr[...]
    himo_ref[...] = him_scr[...]
    y = jnp.concatenate(ys, axis=1) + dsk_ref[...] * u
    o = _dot(_gelu_tanh(y), wout_ref[...])
    out = o[:, :D_MODEL] * jax.nn.sigmoid(o[:, D_MODEL:])
    o_ref[...] = _layer_norm(ALPHA * x + out, lng_ref[...], lnb_ref[...], LN_EPS)


def _s5(x, h_re0, h_im0, w_in, bbd, cre, cim, a_re, a_im, dsk, w_out, lng, lnb, *, bq, tm, lb):
    n = x.shape[0]
    assert n % tm == 0 and tm % bq == 0
    ins = [x, w_in, bbd, cre, cim, a_re, a_im, dsk, w_out, lng, lnb, h_re0, h_im0]
    st = jax.ShapeDtypeStruct((bq, D_SSM), F32)
    return pl.pallas_call(
        functools.partial(_s5_kernel, bq=bq, tm=tm, lb=lb),
        grid=(n // tm,),
        in_specs=[_rows(tm, D_MODEL)] + [_full(a.shape) for a in ins[1:]],
        out_specs=[_rows(tm, D_MODEL), _full((bq, D_SSM)), _full((bq, D_SSM))],
        out_shape=[jax.ShapeDtypeStruct((n, D_MODEL), F32), st, st],
        scratch_shapes=[pltpu.VMEM((tm, D_SSM), F32), pltpu.VMEM((tm, D_SSM), F32),
                        pltpu.VMEM((bq, D_SSM), F32), pltpu.VMEM((bq, D_SSM), F32)],
        compiler_params=_params(1), name="s5")(*ins)


def _ffn_kernel(x_ref, wup_ref, cw_ref, cb_ref, wdn_ref, lng_ref, lnb_ref, st0_ref, o_ref, sto_ref,
                carry_scr, *, bq, tm):
    @pl.when(pl.program_id(0) == 0)
    def _():
        carry_scr[...] = st0_ref[...]

    x = x_ref[...]
    xb = x.astype(BF16)
    acc = jnp.zeros((tm, D_MODEL), F32)
    for ci in range(D_FF // FFN_CHUNK):
        cs = slice(ci * FFN_CHUNK, (ci + 1) * FFN_CHUNK)
        gs = slice(D_FF + ci * FFN_CHUNK, D_FF + (ci + 1) * FFN_CHUNK)
        h = jnp.dot(xb, wup_ref[:, cs], preferred_element_type=F32)
        gate = jnp.dot(xb, wup_ref[:, gs], preferred_element_type=F32)
        old = carry_scr[:, cs]
        h_m1 = jnp.concatenate([old[bq:], h[:tm - bq]], axis=0)
        h_m2 = jnp.concatenate([old, h[:tm - 2 * bq]], axis=0)
        cv = cw_ref[0:1, cs] * h_m2 + cw_ref[1:2, cs] * h_m1 + cw_ref[2:3, cs] * h + cb_ref[:, cs]
        act = cv * jax.nn.sigmoid(cv) * gate
        acc = acc + jnp.dot(act.astype(BF16), wdn_ref[cs, :], preferred_element_type=F32)
        carry_scr[:, cs] = h[tm - 2 * bq:]
    o_ref[...] = _layer_norm(ALPHA * x + acc, lng_ref[...], lnb_ref[...], LN_EPS)
    sto_ref[...] = carry_scr[...]


def _ffn(x, st0, w_up, cw, cb, w_dn, lng, lnb, *, bq, tm):
    n = x.shape[0]
    assert n % tm == 0 and tm >= 2 * bq and D_FF % FFN_CHUNK == 0
    ins = [x, w_up, cw, cb, w_dn, lng, lnb, st0]
    st = jax.ShapeDtypeStruct(st0.shape, F32)
    return pl.pallas_call(
        functools.partial(_ffn_kernel, bq=bq, tm=tm),
        grid=(n // tm,),
        in_specs=[_rows(tm, D_MODEL)] + [_full(a.shape) for a in ins[1:]],
        out_specs=[_rows(tm, D_MODEL), _full(st0.shape)],
        out_shape=[jax.ShapeDtypeStruct((n, D_MODEL), F32), st],
        scratch_shapes=[pltpu.VMEM(st0.shape, F32)],
        compiler_params=_params(1), name="ffn")(*ins)


def _trunk(x, st_conv, st_shift, st_wkv, st_re, st_im, st_ffn, pr, *, bq, t_len, tiles):
    n_ab, n_c = (DEPTH + 1) // 2, DEPTH // 2
    hist = CONV_A_WIDTH - 1
    c = min(WKV_CHUNK, 8 * ((t_len + 7) // 8))
    t_pad = c * ((t_len + c - 1) // c)
    o_conv, o_shift, o_wkv, o_re, o_im, o_ffn = [], [], [], [], [], []

    for i in range(DEPTH):
        j = i // 2
        if i % 2 == 0:
            conv0 = jnp.transpose(st_conv[:, j], (1, 0, 2)).reshape(hist * bq, D_A)
            outs = _ab_in(x, conv0, st_shift[:, j], pr["w_in_ab"][j], pr["conv_a_w"][j], pr["conv_a_b"][j],
                          pr["ln_a_g"][j], pr["ln_a_b"][j], pr["mu_b"][j], pr["w0_b"][j], pr["w2_pad"][j],
                          pr["a0_b"][j], pr["a2_pad"][j], pr["g2_b"][j], pr["k_k_b"][j], pr["k_a_b"][j],
                          pr["seg"], bq=bq, tm=tiles["ab_in"], t_pad=t_pad)
            ya, r, k, v, ld, kk, bv, g, conv_n, shift_n = outs
            yb, wkv_n = _wkv(r, k, v, ld, kk, bv, g, st_wkv[:, j], pr["ln_x_g"][j], pr["ln_x_b"][j],
                             pr["r_k_b"][j], pr["seg"][:WKV_LANES, :WKV_LANES], c=c, sb=tiles["wkv_sb"])
            x = _ab_out(x, ya, yb, pr["w_out_ab"][j], pr["ln1_g"][i], pr["ln1_b"][i], bq=bq, tm=tiles["ab_out"])
            o_conv.append(jnp.transpose(conv_n.reshape(hist, bq, D_A), (1, 0, 2)))
            o_shift.append(shift_n)
            o_wkv.append(wkv_n)
        else:
            x, h_re, h_im = _s5(x, st_re[:, j].reshape(bq, D_SSM), st_im[:, j].reshape(bq, D_SSM),
                                pr["w_in_c"][j], pr["bbd"][j], pr["cre"][j], pr["cim"][j], pr["a_re"][j],
                                pr["a_im"][j], pr["d_skip"][j], pr["w_out_c"][j], pr["ln1_g"][i],
                                pr["ln1_b"][i], bq=bq, tm=tiles["s5"], lb=tiles["s5_lb"])
            o_re.append(h_re.reshape(bq, G_C, P_C))
            o_im.append(h_im.reshape(bq, G_C, P_C))
        ffn0 = jnp.transpose(st_ffn[:, i], (1, 0, 2)).reshape((FFN_CONV_WIDTH - 1) * bq, D_FF)
        x, ffn_n = _ffn(x, ffn0, pr["w_up"][i], pr["conv_f_w"][i], pr["conv_f_b"][i], pr["w_down"][i],
                        pr["ln2_g"][i], pr["ln2_b"][i], bq=bq, tm=tiles["ffn"])
        o_ffn.append(jnp.transpose(ffn_n.reshape(FFN_CONV_WIDTH - 1, bq, D_FF), (1, 0, 2)))
    stack = lambda xs: jnp.stack(xs, 1)
    return x, stack(o_conv), stack(o_shift), stack(o_wkv), stack(o_re), stack(o_im), stack(o_ffn)


def _prepare(w_in_ab, conv_a_w, conv_a_b, ln_a_g, ln_a_b, mu_b, w0_b, w2_b, a0_b, a2_b, g2_b, k_k_b, k_a_b,
             r_k_b, ln_x_g, ln_x_b, w_out_ab, w_in_c, lam_re, lam_im, log_dt, b_re, b_im, c_re, c_im, d_skip,
             w_out_c, w_up, conv_f_w, conv_f_b, w_down, ln1_g, ln1_b, ln2_g, ln2_b):
    n_ab, n_c = w_in_ab.shape[0], w_in_c.shape[0]
    row = lambda a: a.reshape(a.shape[0], 1, -1)
    zeros = jnp.zeros((n_ab, LORA_W, D_B), F32)
    eye = jnp.eye(G_SLAB, dtype=F32)
    coefs = [_s5_coef(lam_re[j], lam_im[j], log_dt[j], b_re[j], b_im[j]) for j in range(n_c)]

    def drive(bb):
        bb = bb.reshape(C_GROUP, N_SLAB, G_SLAB, P_C)
        return jnp.einsum("csgp,gh->sgchp", bb, eye).reshape(N_SLAB, SLAB, SSM_SLAB)

    def readout(cc):
        cc = cc.reshape(N_SLAB, G_SLAB, C_GROUP, P_C)
        return jnp.einsum("sgcp,gh->shpgc", cc, eye).reshape(N_SLAB, SSM_SLAB, SLAB)

    return {
        "w_in_ab": w_in_ab.astype(BF16), "conv_a_w": conv_a_w, "conv_a_b": row(conv_a_b),
        "ln_a_g": row(ln_a_g), "ln_a_b": row(ln_a_b), "mu_b": row(mu_b), "w0_b": row(w0_b),
        "w2_pad": jnp.concatenate([w2_b, zeros], axis=1).astype(BF16), "a0_b": row(a0_b),
        "a2_pad": jnp.concatenate([zeros, a2_b], axis=1).astype(BF16), "g2_b": g2_b.astype(BF16),
        "k_k_b": row(k_k_b), "k_a_b": row(k_a_b), "r_k_b": row(r_k_b), "ln_x_g": row(ln_x_g),
        "ln_x_b": row(ln_x_b), "w_out_ab": w_out_ab.astype(BF16),
        "seg": jnp.kron(jnp.eye(H_B, dtype=F32), jnp.ones((HEAD_B, HEAD_B), F32)).astype(BF16),
        "w_in_c": w_in_c.astype(BF16),
        "bbd": jnp.stack([jnp.concatenate([drive(cf[2]), drive(cf[3])], axis=-1) for cf in coefs]).astype(BF16),
        "cre": jnp.stack([readout(c_re[j]) for j in range(n_c)]).astype(BF16),
        "cim": jnp.stack([readout(c_im[j]) for j in range(n_c)]).astype(BF16),
        "a_re": jnp.stack([cf[0].reshape(1, D_SSM) for cf in coefs]),
        "a_im": jnp.stack([cf[1].reshape(1, D_SSM) for cf in coefs]),
        "d_skip": row(d_skip), "w_out_c": w_out_c.astype(BF16),
        "w_up": w_up.astype(BF16), "conv_f_w": conv_f_w, "conv_f_b": row(conv_f_b),
        "w_down": w_down.astype(BF16),
        "ln1_g": row(ln1_g), "ln1_b": row(ln1_b), "ln2_g": row(ln2_g), "ln2_b": row(ln2_b),
    }


PROMPT_TILES = {"ab_in": 256, "wkv_sb": 8, "ab_out": 512, "s5": 256, "s5_lb": 512, "ffn": 512}
SAMPLE_TILES = {"ab_in": 512, "wkv_sb": 8, "ab_out": 512, "s5": 512, "s5_lb": 128, "ffn": 512}


def _run_group(x, states, pr, tiles):
    bq, t_len, _ = x.shape
    xt = jnp.transpose(x, (1, 0, 2)).reshape(t_len * bq, D_MODEL)
    y, *new_states = _trunk(xt, *states, pr, bq=bq, t_len=t_len, tiles=tiles)
    return (jnp.transpose(y.reshape(t_len, bq, D_MODEL), (1, 0, 2)), *new_states)


def kernel(x_prompt, x_sample, state_conv_a, state_shift_b, state_wkv_b, state_ssm_re, state_ssm_im,
           state_conv_ffn, w_in_ab, conv_a_w, conv_a_b, ln_a_g, ln_a_b, mu_b, w0_b, w2_b, a0_b, a2_b, g2_b,
           k_k_b, k_a_b, r_k_b, ln_x_g, ln_x_b, w_out_ab, w_in_c, lam_re, lam_im, log_dt, b_re, b_im, c_re,
           c_im, d_skip, w_out_c, w_up, conv_f_w, conv_f_b, w_down, ln1_g, ln1_b, ln2_g, ln2_b):
    pr = _prepare(w_in_ab, conv_a_w, conv_a_b, ln_a_g, ln_a_b, mu_b, w0_b, w2_b, a0_b, a2_b, g2_b, k_k_b,
                  k_a_b, r_k_b.reshape(r_k_b.shape[0], D_B), ln_x_g, ln_x_b, w_out_ab, w_in_c, lam_re, lam_im,
                  log_dt, b_re, b_im, c_re, c_im, d_skip, w_out_c, w_up, conv_f_w, conv_f_b, w_down, ln1_g,
                  ln1_b, ln2_g, ln2_b)
    sample_states = (state_conv_a, state_shift_b, state_wkv_b, state_ssm_re, state_ssm_im, state_conv_ffn)
    bp = x_prompt.shape[0]
    prompt_states = tuple(jnp.zeros((bp,) + s.shape[1:], s.dtype) for s in sample_states)
    p = _run_group(x_prompt, prompt_states, pr, PROMPT_TILES)
    s = _run_group(x_sample, sample_states, pr, SAMPLE_TILES)
    return (p[0], s[0], p[1], s[1], p[2], s[2], p[3], s[3], p[4], s[4], p[5], s[5], p[6], s[6])
```

```python
import functools
import math

import jax
import jax.numpy as jnp
from jax import lax
from jax.experimental import pallas as pl
from jax.experimental.pallas import tpu as pltpu

F32 = jnp.float32
BF16 = jnp.bfloat16

D_MODEL = 1024
DEPTH = 4
D_A = 512
D_B = 512
HEAD_B = 64
H_B = D_B // HEAD_B
LORA_W = 64
LORA_A = 64
LORA_G = 128
D_BP = 3 * D_B + LORA_W + LORA_A + LORA_G
D_IN_AB = 2 * D_A + D_BP
CONV_A_WIDTH = 31
C_GROUP = 16
G_C = D_MODEL // C_GROUP
P_C = 64
D_SSM = G_C * P_C
SLAB = 128
N_SLAB = D_MODEL // SLAB
G_SLAB = SLAB // C_GROUP
SSM_SLAB = G_SLAB * P_C
D_FF = 2816
FFN_CONV_WIDTH = 3
ALPHA = (2 * DEPTH) ** 0.25
LN_EPS = 1e-5
GN_EPS = HEAD_B * 1e-5

VMEM_LIMIT_BYTES = 56 * 1024 * 1024
FFN_CHUNK = 256
WKV_CHUNK = 64
WKV_GROUP = 4
WKV_LANES = WKV_GROUP * HEAD_B
SUBLANES = 8
CONV_ROWS = 32


def _params(n_grid):
    return pltpu.CompilerParams(dimension_semantics=("arbitrary",) * n_grid,
                                vmem_limit_bytes=VMEM_LIMIT_BYTES)


def _full(shape):
    nd = len(shape)
    return pl.BlockSpec(shape, lambda *_: (0,) * nd, pipeline_mode=pl.Buffered(1))


def _spec(a):
    if not isinstance(a, tuple):
        return _full(a.shape)
    arr, layer = a
    nd = arr.ndim - 1
    return pl.BlockSpec((None,) + arr.shape[1:], lambda *_: (layer,) + (0,) * nd, pipeline_mode=pl.Buffered(1))


def _arr(a):
    return a[0] if isinstance(a, tuple) else a


def _rows(tm, width):
    return pl.BlockSpec((tm, width), lambda i: (i, 0))


def _dot(a, b):
    return jnp.dot(a.astype(BF16), b.astype(BF16), preferred_element_type=F32)


def _split(a):
    hi = a.astype(BF16)
    lo = (a - hi.astype(F32)).astype(BF16)
    return hi, lo


def _dg(a, b, ca, cb):
    return lax.dot_general(a, b, (((ca,), (cb,)), ((), ())), preferred_element_type=F32)


def _layer_norm(x, g, b, eps):
    xc = x - jnp.mean(x, -1, keepdims=True)
    var = jnp.mean(xc * xc, -1, keepdims=True)
    return xc * lax.rsqrt(var + eps) * g + b


def _ab_in_kernel(x_ref, w_ref, cw_ref, cb_ref, lng_ref, lnb_ref, mu_ref, w0_ref, w2_ref, a0_ref,
                  a2_ref, g2_ref, kkw_ref, kaw_ref, seg_ref, conv0_ref, shift0_ref,
                  ya_ref, r_ref, k_ref, v_ref, ld_ref, kk_ref, b_ref, g_ref, convo_ref, shifto_ref,
                  ext_scr, sh_scr, *, bq, tm):
    hist = (CONV_A_WIDTH - 1) * bq

    @pl.when(pl.program_id(0) == 0)
    def _():
        ext_scr[0:hist, :] = conv0_ref[...]
        sh_scr[0:bq, :] = shift0_ref[...]

    p = _dot(x_ref[...], w_ref[...])

    ext_scr[hist:hist + tm, :] = p[:, :D_A] * jax.nn.sigmoid(p[:, D_A:2 * D_A])
    tiles = CONV_ROWS // SUBLANES
    for r0 in range(0, tm, CONV_ROWS):
        acc = jnp.zeros((tiles, SUBLANES, D_A), F32) + cb_ref[...]
        for j in range(CONV_A_WIDTH):
            taps = ext_scr[r0 + j * bq:r0 + j * bq + CONV_ROWS, :].reshape(tiles, SUBLANES, D_A)
            acc = acc + cw_ref[j] * taps
        ya = _layer_norm(acc.reshape(CONV_ROWS, D_A), lng_ref[...], lnb_ref[...], LN_EPS)
        ya_ref[r0:r0 + CONV_ROWS, :] = ya * jax.nn.sigmoid(ya)
    new_hist = ext_scr[tm:tm + hist, :]
    convo_ref[...] = new_hist
    ext_scr[0:hist, :] = new_hist

    pb = p[:, 2 * D_A:]
    sh_scr[bq:bq + tm, :] = pb
    prev = sh_scr[0:tm, :]
    last = pb[tm - bq:, :]
    sh_scr[0:bq, :] = last
    shifto_ref[...] = last
    q = pb + (prev - pb) * mu_ref[...]
    r = q[:, :D_B]
    k = q[:, D_B:2 * D_B]
    v = q[:, 2 * D_B:3 * D_B]
    wa = q[:, 3 * D_B:3 * D_B + LORA_W + LORA_A]
    gl = q[:, 3 * D_B + LORA_W + LORA_A:]
    w = -jax.nn.softplus(-(w0_ref[...] + _dot(jnp.tanh(wa), w2_ref[...]))) - 0.5
    a = jax.nn.sigmoid(a0_ref[...] + _dot(wa, a2_ref[...]))
    kk = k * kkw_ref[...]
    sq_hi, sq_lo = _split(kk * kk)
    ssq = jnp.dot(sq_hi, seg_ref[...], preferred_element_type=F32) + jnp.dot(
        sq_lo, seg_ref[...], preferred_element_type=F32)
    kk = kk * lax.rsqrt(jnp.maximum(ssq, 1e-24))
    def put(ref, val):
        val = val.reshape(tm // bq, bq, D_B)
        pad = ref.shape[1] - tm // bq
        if pad:
            val = jnp.concatenate([val, jnp.zeros((pad, bq, D_B), F32)], axis=0)
        ref[...] = jnp.swapaxes(val, 0, 1)

    g_ref[...] = _dot(jax.nn.sigmoid(gl), g2_ref[...])
    put(r_ref, r)
    put(k_ref, k * (1.0 + (a - 1.0) * kaw_ref[...]))
    put(v_ref, v)
    put(ld_ref, -jnp.exp(w))
    put(kk_ref, kk)
    put(b_ref, kk * a)


def _ab_in(x, conv0, shift0, w_in, cw, cb, lng, lnb, mu, w0, w2p, a0, a2p, g2, kkw, kaw, seg, *, bq, tm, t_pad):
    n = x.shape[0]
    hist = (CONV_A_WIDTH - 1) * bq
    nt = n // tm
    steps = tm // bq
    assert n % tm == 0 and tm % bq == 0 and tm % CONV_ROWS == 0 and (nt == 1 or tm >= hist)
    t_blk = steps if nt > 1 else t_pad
    assert nt * t_blk == t_pad
    seq = jax.ShapeDtypeStruct((bq, t_pad, D_B), F32)
    seq_spec = pl.BlockSpec((bq, t_blk, D_B), lambda i: (0, i, 0))
    ins = [x, w_in, cw, cb, lng, lnb, mu, w0, w2p, a0, a2p, g2, kkw, kaw, seg, conv0, shift0]
    in_specs = [_rows(tm, D_MODEL)] + [_spec(a) for a in ins[1:]]
    return pl.pallas_call(
        functools.partial(_ab_in_kernel, bq=bq, tm=tm),
        grid=(nt,),
        in_specs=in_specs,
        out_specs=[_rows(tm, D_A)] + [seq_spec] * 6 + [_rows(tm, D_B), _full((hist, D_A)), _full((bq, D_BP))],
        out_shape=[jax.ShapeDtypeStruct((n, D_A), F32)] + [seq] * 6 + [jax.ShapeDtypeStruct((n, D_B), F32)]
        + [jax.ShapeDtypeStruct((hist, D_A), F32), jax.ShapeDtypeStruct((bq, D_BP), F32)],
        scratch_shapes=[pltpu.VMEM((hist + tm, D_A), F32), pltpu.VMEM((bq + tm, D_BP), F32)],
        compiler_params=_params(1), name="ab_in")(*map(_arr, ins))


def _head_blocks(y, width):
    lane_head = lax.broadcasted_iota(jnp.int32, y.shape, 1) // width
    return jnp.concatenate([jnp.where(lane_head == h, y, jnp.zeros_like(y)) for h in range(WKV_GROUP)], axis=0)


def _per_head(x, y, width):
    return _dg(x.astype(BF16), _head_blocks(y.astype(BF16), width), 1, 0)


def _per_head_hp(x, y, width):
    xh, xl = _split(x)
    yh, yl = _split(y)
    m = x.shape[0]
    both = _dg(jnp.concatenate([xh, xl], axis=0), _head_blocks(yh, width), 1, 0)
    return both[:m] + (both[m:] + _dg(xh, _head_blocks(yl, width), 1, 0))


def _wkv_kernel(r_ref, k_ref, v_ref, ld_ref, kk_ref, b_ref, s0_ref, lnxg_ref, lnxb_ref, rk_ref, seg_ref,
                y_ref, so_ref, s_scr, *, c, sb, n_sq):
    n_grp = H_B // WKV_GROUP
    gc = WKV_GROUP * c
    blk_r = lax.broadcasted_iota(jnp.int32, (WKV_LANES, WKV_LANES), 0) // HEAD_B
    blk_c = lax.broadcasted_iota(jnp.int32, (WKV_LANES, WKV_LANES), 1) // HEAD_B
    same_head = blk_r == blk_c

    @pl.when(pl.program_id(1) == 0)
    def _():
        for q in range(sb):
            for gi in range(n_grp):
                s = s0_ref[q, gi * WKV_GROUP:(gi + 1) * WKV_GROUP].reshape(WKV_LANES, HEAD_B)
                s_scr[q, gi] = jnp.where(same_head, jnp.concatenate([s] * WKV_GROUP, axis=1), 0.0)

    row = lax.broadcasted_iota(jnp.int32, (c, c), 0)
    col = lax.broadcasted_iota(jnp.int32, (c, c), 1)
    tri = jnp.where(row >= col, 1.0, 0.0).astype(BF16)
    row_g = lax.broadcasted_iota(jnp.int32, (c, gc), 0)
    col_g = lax.broadcasted_iota(jnp.int32, (c, gc), 1) % c
    incl = row_g >= col_g
    strict = row_g > col_g
    eye = jnp.where(row_g == col_g, 1.0, 0.0).astype(F32)

    left, right, v_g, bk_g, p_end, s_prev = [], [], [], [], [], []
    for q in range(sb):
        ld = ld_ref[q]
        ld_hi = ld.astype(BF16)
        ld_rem = ld - ld_hi.astype(F32)
        ld_mid = ld_rem.astype(BF16)
        ld_lo = (ld_rem - ld_mid.astype(F32)).astype(BF16)
        cum = (jnp.dot(tri, ld_hi, preferred_element_type=F32)
               + (jnp.dot(tri, ld_mid, preferred_element_type=F32)
                  + jnp.dot(tri, ld_lo, preferred_element_type=F32)))
        p_in = jnp.exp(cum)
        p_inv = jnp.exp(-cum)
        kk_t = kk_ref[q] * jnp.exp(cum - ld)
        r_t = r_ref[q] * p_in
        b_t = b_ref[q] * p_inv
        k_t = k_ref[q] * p_inv
        v_all = v_ref[q]
        for gi in range(n_grp):
            gl = slice(gi * WKV_LANES, (gi + 1) * WKV_LANES)
            left.append(jnp.concatenate([kk_t[:, gl], r_t[:, gl]], axis=0).astype(BF16))
            b_bf, k_bf = b_t[:, gl].astype(BF16), k_t[:, gl].astype(BF16)
            right.append(jnp.concatenate([_head_blocks(b_bf, HEAD_B), _head_blocks(k_bf, HEAD_B)], axis=0))
            bk_g.append(jnp.concatenate([b_bf, k_bf], axis=0))
            v_g.append(v_all[:, gl])
            p_end.append(p_in[c - 1:c, gl])
            s_prev.append(s_scr[q, gi])

    chains = range(sb * n_grp)
    gram = [_dg(left[i], right[i], 1, 1) for i in chains]
    a_b = [jnp.where(strict, gram[i][:c, :gc], 0.0) for i in chains]
    a_k = [jnp.where(strict, gram[i][:c, gc:], 0.0) for i in chains]
    a_rb = [jnp.where(incl, gram[i][c:, :gc], 0.0) for i in chains]
    a_rk = [jnp.where(incl, gram[i][c:, gc:], 0.0) for i in chains]
    gs = [_dg(left[i], s_prev[i].astype(BF16), 1, 1) for i in chains]
    av = [_per_head(jnp.concatenate([a_k[i], a_rk[i]], axis=0), v_g[i], HEAD_B) for i in chains]
    rhs = [gs[i][:c] + av[i][:c] for i in chains]
    y_v = [gs[i][c:] + av[i][c:] for i in chains]
    n_pow = [-a_b[i] for i in chains]
    t0 = [eye + n_pow[i] for i in chains]
    if n_sq:
        n_pow = [_per_head(n_pow[i], n_pow[i], c) for i in chains]
        for _ in range(n_sq - 1):
            both = [_per_head(jnp.concatenate([t0[i], n_pow[i]], axis=0), n_pow[i], c) for i in chains]
            t0 = [t0[i] + both[i][:c] for i in chains]
            n_pow = [both[i][c:] for i in chains]
        t0 = [t0[i] + _per_head(t0[i], n_pow[i], c) for i in chains]
    res = [eye - t0[i] - _per_head_hp(a_b[i], t0[i], c) for i in chains]
    z = [rhs[i] + _per_head(res[i], rhs[i], HEAD_B) for i in chains]
    u = [-_per_head_hp(t0[i], z[i], HEAD_B) for i in chains]
    y = [y_v[i] + _per_head(a_rb[i], u[i], HEAD_B) for i in chains]
    upd = [_dg(jnp.concatenate([u[i], v_g[i]], axis=0).astype(BF16), bk_g[i], 0, 0) for i in chains]
    s_new = [(s_prev[i] + jnp.where(same_head, upd[i], 0.0)) * p_end[i] for i in chains]
    for i in chains:
        s_scr[i // n_grp, i % n_grp] = s_new[i]

    rows = sb * c

    def head_sum(x):
        hi, lo = _split(x)
        parts = [hi[:, :WKV_LANES], lo[:, :WKV_LANES], hi[:, WKV_LANES:], lo[:, WKV_LANES:]]
        s = jnp.dot(jnp.concatenate(parts, axis=0), seg_ref[...], preferred_element_type=F32)
        return jnp.concatenate([s[:rows] + s[rows:2 * rows], s[2 * rows:3 * rows] + s[3 * rows:]], axis=1)

    flat = lambda ref: ref[...].reshape(rows, D_B)
    y_all = jnp.concatenate([jnp.concatenate(y[q * n_grp:(q + 1) * n_grp], axis=1) for q in range(sb)], axis=0)
    yc = y_all - head_sum(y_all) * (1.0 / HEAD_B)
    yn = yc * lax.rsqrt(head_sum(yc * yc) * (1.0 / HEAD_B) + GN_EPS)
    bonus = head_sum(flat(r_ref) * flat(k_ref) * rk_ref[...]) * flat(v_ref)
    y_ref[...] = (yn * lnxg_ref[...] + lnxb_ref[...] + bonus).reshape(sb, c, D_B)

    @pl.when(pl.program_id(1) == pl.num_programs(1) - 1)
    def _():
        for q in range(sb):
            for gi in range(n_grp):
                for h in range(WKV_GROUP):
                    hl = slice(h * HEAD_B, (h + 1) * HEAD_B)
                    so_ref[q, gi * WKV_GROUP + h] = s_scr[q, gi, hl, hl]


def _wkv(r, k, v, ld, kk, b, s0, lnxg, lnxb, rk, seg, *, c, sb):
    nb, t, _ = r.shape
    assert t % c == 0 and nb % sb == 0
    n_sq = max(int(math.ceil(math.log2(c))) - 1, 0)
    seq = pl.BlockSpec((sb, c, D_B), lambda i, j: (i, j, 0))
    st = pl.BlockSpec((sb, H_B, HEAD_B, HEAD_B), lambda i, j: (i, 0, 0, 0))
    return pl.pallas_call(
        functools.partial(_wkv_kernel, c=c, sb=sb, n_sq=n_sq),
        grid=(nb // sb, t // c),
        in_specs=[seq] * 6 + [st, _spec(lnxg), _spec(lnxb), _spec(rk), _spec(seg)],
        out_specs=[seq, st],
        out_shape=[jax.ShapeDtypeStruct((nb, t, D_B), F32),
                   jax.ShapeDtypeStruct((nb, H_B, HEAD_B, HEAD_B), F32)],
        scratch_shapes=[pltpu.VMEM((sb, H_B // WKV_GROUP, WKV_LANES, WKV_LANES), F32)],
        compiler_params=_params(2), name="wkv")(r, k, v, ld, kk, b, s0, _arr(lnxg), _arr(lnxb), _arr(rk), seg)


def _ab_out_kernel(x_ref, ya_ref, yb_ref, gate_ref, w_ref, g_ref, b_ref, o_ref, *, bq, tm):
    yb = jnp.swapaxes(yb_ref[...], 0, 1)[:tm // bq].reshape(tm, D_B)
    out = _dot(ya_ref[...], w_ref[0:D_A, :]) + _dot(yb * gate_ref[...], w_ref[D_A:, :])
    o_ref[...] = _layer_norm(ALPHA * x_ref[...] + out, g_ref[...], b_ref[...], LN_EPS)


def _ab_out(x, ya, yb, gate, w_out, g, b, *, bq, tm):
    n = x.shape[0]
    nt = n // tm
    t_blk = tm // bq if nt > 1 else yb.shape[1]
    assert n % tm == 0 and nt * t_blk == yb.shape[1]
    return pl.pallas_call(
        functools.partial(_ab_out_kernel, bq=bq, tm=tm), grid=(nt,),
        in_specs=[_rows(tm, D_MODEL), _rows(tm, D_A), pl.BlockSpec((bq, t_blk, D_B), lambda i: (0, i, 0)),
                  _rows(tm, D_B), _spec(w_out), _spec(g), _spec(b)],
        out_specs=_rows(tm, D_MODEL), out_shape=jax.ShapeDtypeStruct((n, D_MODEL), F32),
        compiler_params=_params(1), name="ab_out")(x, ya, yb, gate, _arr(w_out), _arr(g), _arr(b))


def _s5_coef_kernel(lr_ref, li_ref, ldt_ref, br_ref, bi_ref, are_ref, aim_ref, bbr_ref, bbi_ref):
    lr, li = lr_ref[...], li_ref[...]
    dt = jnp.exp(ldt_ref[...])
    mag = jnp.exp(lr * dt)
    ab_re, ab_im = mag * jnp.cos(li * dt), mag * jnp.sin(li * dt)
    den = lr * lr + li * li
    nr, ni = ab_re - 1.0, ab_im
    f_re, f_im = (nr * lr + ni * li) / den, (ni * lr - nr * li) / den
    are_ref[...] = ab_re
    aim_ref[...] = ab_im
    for ch in range(C_GROUP):
        bbr_ref[ch] = f_re * br_ref[ch] - f_im * bi_ref[ch]
        bbi_ref[ch] = f_re * bi_ref[ch] + f_im * br_ref[ch]


def _s5_coef(lam_re, lam_im, log_dt, b_re, b_im):
    br = jnp.transpose(b_re, (2, 0, 1))
    bi = jnp.transpose(b_im, (2, 0, 1))
    gp = jax.ShapeDtypeStruct((G_C, P_C), F32)
    cgp = jax.ShapeDtypeStruct((C_GROUP, G_C, P_C), F32)
    return pl.pallas_call(_s5_coef_kernel, out_shape=[gp, gp, cgp, cgp], name="s5_coef")(
        lam_re, lam_im, log_dt[:, None], br, bi)


def _gelu_tanh(x):
    return 0.5 * x * (1.0 + jnp.tanh(math.sqrt(2.0 / math.pi) * (x + 0.044715 * (x * x * x))))


def _s5_kernel(x_ref, win_ref, bbd_ref, cre_ref, cim_ref, are_ref, aim_ref, dsk_ref, wout_ref, lng_ref,
               lnb_ref, hre0_ref, him0_ref, o_ref, hreo_ref, himo_ref, bre_scr, bim_scr, hre_scr, him_scr,
               *, bq, tm, lb):
    @pl.when(pl.program_id(0) == 0)
    def _():
        hre_scr[...] = hre0_ref[...]
        him_scr[...] = him0_ref[...]

    x = x_ref[...]
    u = _dot(x, win_ref[...])
    ub = u.astype(BF16)
    ys = []
    for j in range(N_SLAB):
        slab = slice(j * SSM_SLAB, (j + 1) * SSM_SLAB)
        bu = jnp.dot(ub[:, j * SLAB:(j + 1) * SLAB], bbd_ref[j], preferred_element_type=F32)
        bre_scr[:, slab] = bu[:, :SSM_SLAB]
        bim_scr[:, slab] = bu[:, SSM_SLAB:]
        for blk in range(SSM_SLAB // lb):
            cs = slice(j * SSM_SLAB + blk * lb, j * SSM_SLAB + (blk + 1) * lb)
            a_re, a_im = are_ref[:, cs], aim_ref[:, cs]
            h_re, h_im = hre_scr[:, cs], him_scr[:, cs]
            for t in range(tm // bq):
                rows = slice(t * bq, (t + 1) * bq)
                h_re, h_im = (a_re * h_re - a_im * h_im + bre_scr[rows, cs],
                              a_re * h_im + a_im * h_re + bim_scr[rows, cs])
                bre_scr[rows, cs] = h_re
                bim_scr[rows, cs] = h_im
            hre_scr[:, cs] = h_re
            him_scr[:, cs] = h_im
        ys.append(_dot(bre_scr[:, slab], cre_ref[j]) - _dot(bim_scr[:, slab], cim_ref[j]))
    hreo_ref[...] = hre_scr[...]
    himo_ref[...] = him_scr[...]
    y = jnp.concatenate(ys, axis=1) + dsk_ref[...] * u
    o = _dot(_gelu_tanh(y), wout_ref[...])
    out = o[:, :D_MODEL] * jax.nn.sigmoid(o[:, D_MODEL:])
    o_ref[...] = _layer_norm(ALPHA * x + out, lng_ref[...], lnb_ref[...], LN_EPS)


def _s5(x, h_re0, h_im0, w_in, bbd, cre, cim, a_re, a_im, dsk, w_out, lng, lnb, *, bq, tm, lb):
    n = x.shape[0]
    assert n % tm == 0 and tm % bq == 0
    ins = [x, w_in, bbd, cre, cim, a_re, a_im, dsk, w_out, lng, lnb, h_re0, h_im0]
    st = jax.ShapeDtypeStruct((bq, D_SSM), F32)
    return pl.pallas_call(
        functools.partial(_s5_kernel, bq=bq, tm=tm, lb=lb),
        grid=(n // tm,),
        in_specs=[_rows(tm, D_MODEL)] + [_spec(a) for a in ins[1:]],
        out_specs=[_rows(tm, D_MODEL), _full((bq, D_SSM)), _full((bq, D_SSM))],
        out_shape=[jax.ShapeDtypeStruct((n, D_MODEL), F32), st, st],
        scratch_shapes=[pltpu.VMEM((tm, D_SSM), F32), pltpu.VMEM((tm, D_SSM), F32),
                        pltpu.VMEM((bq, D_SSM), F32), pltpu.VMEM((bq, D_SSM), F32)],
        compiler_params=_params(1), name="s5")(*map(_arr, ins))


def _ffn_kernel(x_ref, wup_ref, cw_ref, cb_ref, wdn_ref, lng_ref, lnb_ref, st0_ref, o_ref, sto_ref,
                carry_scr, *, bq, tm, seq_out):
    @pl.when(pl.program_id(0) == 0)
    def _():
        carry_scr[...] = st0_ref[...]

    x = x_ref[...]
    xb = x.astype(BF16)
    acc = jnp.zeros((tm, D_MODEL), F32)
    for ci in range(D_FF // FFN_CHUNK):
        cs = slice(ci * FFN_CHUNK, (ci + 1) * FFN_CHUNK)
        gs = slice(D_FF + ci * FFN_CHUNK, D_FF + (ci + 1) * FFN_CHUNK)
        h = jnp.dot(xb, wup_ref[:, cs], preferred_element_type=F32)
        gate = jnp.dot(xb, wup_ref[:, gs], preferred_element_type=F32)
        old = carry_scr[:, cs]
        h_m1 = jnp.concatenate([old[bq:], h[:tm - bq]], axis=0)
        h_m2 = jnp.concatenate([old, h[:tm - 2 * bq]], axis=0)
        cv = cw_ref[0:1, cs] * h_m2 + cw_ref[1:2, cs] * h_m1 + cw_ref[2:3, cs] * h + cb_ref[:, cs]
        act = cv * jax.nn.sigmoid(cv) * gate
        acc = acc + jnp.dot(act.astype(BF16), wdn_ref[cs, :], preferred_element_type=F32)
        carry_scr[:, cs] = h[tm - 2 * bq:]
    out = _layer_norm(ALPHA * x + acc, lng_ref[...], lnb_ref[...], LN_EPS)
    if seq_out:
        o_ref[...] = jnp.swapaxes(out.reshape(tm // bq, bq, D_MODEL), 0, 1)
    else:
        o_ref[...] = out
    sto_ref[...] = carry_scr[...]


def _ffn(x, st0, w_up, cw, cb, w_dn, lng, lnb, *, bq, tm, seq_out):
    n = x.shape[0]
    assert n % tm == 0 and tm >= 2 * bq and D_FF % FFN_CHUNK == 0
    ins = [x, w_up, cw, cb, w_dn, lng, lnb, st0]
    st = jax.ShapeDtypeStruct(st0.shape, F32)
    if seq_out:
        o_shape = jax.ShapeDtypeStruct((bq, n // bq, D_MODEL), F32)
        o_spec = pl.BlockSpec((bq, tm // bq, D_MODEL), lambda i: (0, i, 0))
    else:
        o_shape, o_spec = jax.ShapeDtypeStruct((n, D_MODEL), F32), _rows(tm, D_MODEL)
    return pl.pallas_call(
        functools.partial(_ffn_kernel, bq=bq, tm=tm, seq_out=seq_out),
        grid=(n // tm,),
        in_specs=[_rows(tm, D_MODEL)] + [_spec(a) for a in ins[1:]],
        out_specs=[o_spec, _full(st0.shape)],
        out_shape=[o_shape, st],
        scratch_shapes=[pltpu.VMEM(st0.shape, F32)],
        compiler_params=_params(1), name="ffn")(*map(_arr, ins))


def _trunk(x, st_conv, st_shift, st_wkv, st_re, st_im, st_ffn, pr, *, bq, t_len, tiles):
    hist = CONV_A_WIDTH - 1
    c = min(WKV_CHUNK, 8 * ((t_len + 7) // 8))
    t_pad = c * ((t_len + c - 1) // c)
    o_conv, o_shift, o_wkv, o_re, o_im, o_ffn = [], [], [], [], [], []
    at = lambda name, layer: (pr[name], layer)

    for i in range(DEPTH):
        j = i // 2
        if i % 2 == 0:
            conv0 = jnp.transpose(st_conv[:, j], (1, 0, 2)).reshape(hist * bq, D_A)
            outs = _ab_in(x, conv0, st_shift[:, j], *(at(n, j) for n in AB_IN_PARAMS), pr["seg"],
                          bq=bq, tm=tiles["ab_in"], t_pad=t_pad)
            ya, r, k, v, ld, kk, bv, gate, conv_n, shift_n = outs
            yb, wkv_n = _wkv(r, k, v, ld, kk, bv, st_wkv[:, j], at("ln_x_g", j), at("ln_x_b", j), at("r_k_b", j),
                             pr["seg_group"], c=c, sb=tiles["wkv_sb"])
            x = _ab_out(x, ya, yb, gate, at("w_out_ab", j), at("ln1_g", i), at("ln1_b", i), bq=bq,
                        tm=tiles["ab_out"])
            o_conv.append(jnp.transpose(conv_n.reshape(hist, bq, D_A), (1, 0, 2)))
            o_shift.append(shift_n)
            o_wkv.append(wkv_n)
        else:
            x, h_re, h_im = _s5(x, st_re[:, j].reshape(bq, D_SSM), st_im[:, j].reshape(bq, D_SSM),
                                *(at(n, j) for n in S5_PARAMS), at("ln1_g", i), at("ln1_b", i),
                                bq=bq, tm=tiles["s5"], lb=tiles["s5_lb"])
            o_re.append(h_re.reshape(bq, G_C, P_C))
            o_im.append(h_im.reshape(bq, G_C, P_C))
        ffn0 = jnp.transpose(st_ffn[:, i], (1, 0, 2)).reshape((FFN_CONV_WIDTH - 1) * bq, D_FF)
        x, ffn_n = _ffn(x, ffn0, *(at(n, i) for n in FFN_PARAMS), bq=bq, tm=tiles["ffn"],
                        seq_out=i == DEPTH - 1)
        o_ffn.append(jnp.transpose(ffn_n.reshape(FFN_CONV_WIDTH - 1, bq, D_FF), (1, 0, 2)))
    stack = lambda xs: jnp.stack(xs, 1)
    return x, stack(o_conv), stack(o_shift), stack(o_wkv), stack(o_re), stack(o_im), stack(o_ffn)


def _prepare(w_in_ab, conv_a_w, conv_a_b, ln_a_g, ln_a_b, mu_b, w0_b, w2_b, a0_b, a2_b, g2_b, k_k_b, k_a_b,
             r_k_b, ln_x_g, ln_x_b, w_out_ab, w_in_c, lam_re, lam_im, log_dt, b_re, b_im, c_re, c_im, d_skip,
             w_out_c, w_up, conv_f_w, conv_f_b, w_down, ln1_g, ln1_b, ln2_g, ln2_b):
    n_ab, n_c = w_in_ab.shape[0], w_in_c.shape[0]
    row = lambda a: a.reshape(a.shape[0], 1, -1)
    zeros = jnp.zeros((n_ab, LORA_W, D_B), F32)
    eye = jnp.eye(G_SLAB, dtype=F32)
    coefs = [_s5_coef(lam_re[j], lam_im[j], log_dt[j], b_re[j], b_im[j]) for j in range(n_c)]

    def drive(bb):
        bb = bb.reshape(C_GROUP, N_SLAB, G_SLAB, P_C)
        return jnp.einsum("csgp,gh->sgchp", bb, eye).reshape(N_SLAB, SLAB, SSM_SLAB)

    def readout(cc):
        cc = cc.reshape(N_SLAB, G_SLAB, C_GROUP, P_C)
        return jnp.einsum("sgcp,gh->shpgc", cc, eye).reshape(N_SLAB, SSM_SLAB, SLAB)

    return {
        "w_in_ab": w_in_ab.astype(BF16), "conv_a_b": row(conv_a_b),
        "conv_a_w": jnp.broadcast_to(conv_a_w[:, :, None, :], (n_ab, CONV_A_WIDTH, SUBLANES, D_A)),
        "ln_a_g": row(ln_a_g), "ln_a_b": row(ln_a_b), "mu_b": row(mu_b), "w0_b": row(w0_b),
        "w2_pad": jnp.concatenate([w2_b, zeros], axis=1).astype(BF16), "a0_b": row(a0_b),
        "a2_pad": jnp.concatenate([zeros, a2_b], axis=1).astype(BF16), "g2_b": g2_b.astype(BF16),
        "k_k_b": row(k_k_b), "k_a_b": row(k_a_b), "r_k_b": row(r_k_b), "ln_x_g": row(ln_x_g),
        "ln_x_b": row(ln_x_b), "w_out_ab": w_out_ab.astype(BF16),
        "seg": jnp.kron(jnp.eye(H_B, dtype=F32), jnp.ones((HEAD_B, HEAD_B), F32)).astype(BF16),
        "seg_group": jnp.kron(jnp.eye(WKV_GROUP, dtype=F32), jnp.ones((HEAD_B, HEAD_B), F32)).astype(BF16),
        "w_in_c": w_in_c.astype(BF16),
        "bbd": jnp.stack([jnp.concatenate([drive(cf[2]), drive(cf[3])], axis=-1) for cf in coefs]).astype(BF16),
        "cre": jnp.stack([readout(c_re[j]) for j in range(n_c)]).astype(BF16),
        "cim": jnp.stack([readout(c_im[j]) for j in range(n_c)]).astype(BF16),
        "a_re": jnp.stack([cf[0].reshape(1, D_SSM) for cf in coefs]),
        "a_im": jnp.stack([cf[1].reshape(1, D_SSM) for cf in coefs]),
        "d_skip": row(d_skip), "w_out_c": w_out_c.astype(BF16),
        "w_up": w_up.astype(BF16), "conv_f_w": conv_f_w, "conv_f_b": row(conv_f_b),
        "w_down": w_down.astype(BF16),
        "ln1_g": row(ln1_g), "ln1_b": row(ln1_b), "ln2_g": row(ln2_g), "ln2_b": row(ln2_b),
    }


AB_IN_PARAMS = ("w_in_ab", "conv_a_w", "conv_a_b", "ln_a_g", "ln_a_b", "mu_b", "w0_b", "w2_pad", "a0_b", "a2_pad",
                "g2_b", "k_k_b", "k_a_b")
S5_PARAMS = ("w_in_c", "bbd", "cre", "cim", "a_re", "a_im", "d_skip", "w_out_c")
FFN_PARAMS = ("w_up", "conv_f_w", "conv_f_b", "w_down", "ln2_g", "ln2_b")
PROMPT_TILES = {"ab_in": 256, "wkv_sb": 8, "ab_out": 512, "s5": 256, "s5_lb": 512, "ffn": 512}
SAMPLE_TILES = {"ab_in": 512, "wkv_sb": 8, "ab_out": 512, "s5": 512, "s5_lb": 128, "ffn": 512}


def _run_group(x, states, pr, tiles):
    bq, t_len, _ = x.shape
    xt = jnp.transpose(x, (1, 0, 2)).reshape(t_len * bq, D_MODEL)
    return _trunk(xt, *states, pr, bq=bq, t_len=t_len, tiles=tiles)


def kernel(x_prompt, x_sample, state_conv_a, state_shift_b, state_wkv_b, state_ssm_re, state_ssm_im,
           state_conv_ffn, w_in_ab, conv_a_w, conv_a_b, ln_a_g, ln_a_b, mu_b, w0_b, w2_b, a0_b, a2_b, g2_b,
           k_k_b, k_a_b, r_k_b, ln_x_g, ln_x_b, w_out_ab, w_in_c, lam_re, lam_im, log_dt, b_re, b_im, c_re,
           c_im, d_skip, w_out_c, w_up, conv_f_w, conv_f_b, w_down, ln1_g, ln1_b, ln2_g, ln2_b):
    pr = _prepare(w_in_ab, conv_a_w, conv_a_b, ln_a_g, ln_a_b, mu_b, w0_b, w2_b, a0_b, a2_b, g2_b, k_k_b,
                  k_a_b, r_k_b.reshape(r_k_b.shape[0], D_B), ln_x_g, ln_x_b, w_out_ab, w_in_c, lam_re, lam_im,
                  log_dt, b_re, b_im, c_re, c_im, d_skip, w_out_c, w_up, conv_f_w, conv_f_b, w_down, ln1_g,
                  ln1_b, ln2_g, ln2_b)
    sample_states = (state_conv_a, state_shift_b, state_wkv_b, state_ssm_re, state_ssm_im, state_conv_ffn)
    bp = x_prompt.shape[0]
    prompt_states = tuple(jnp.zeros((bp,) + s.shape[1:], s.dtype) for s in sample_states)
    p = _run_group(x_prompt, prompt_states, pr, PROMPT_TILES)
    s = _run_group(x_sample, sample_states, pr, SAMPLE_TILES)
    return (p[0], s[0], p[1], s[1], p[2], s[2], p[3], s[3], p[4], s[4], p[5], s[5], p[6], s[6])
```

```python
import functools
import math

import jax
import jax.numpy as jnp
from jax import lax
from jax.experimental import pallas as pl
from jax.experimental.pallas import tpu as pltpu

F32 = jnp.float32
BF16 = jnp.bfloat16

D_MODEL = 1024
DEPTH = 4
D_A = 512
D_B = 512
HEAD_B = 64
H_B = D_B // HEAD_B
LORA_W = 64
LORA_A = 64
LORA_G = 128
D_BP = 3 * D_B + LORA_W + LORA_A + LORA_G
D_IN_AB = 2 * D_A + D_BP
CONV_A_WIDTH = 31
C_GROUP = 16
G_C = D_MODEL // C_GROUP
P_C = 64
D_SSM = G_C * P_C
SLAB = 128
N_SLAB = D_MODEL // SLAB
G_SLAB = SLAB // C_GROUP
SSM_SLAB = G_SLAB * P_C
D_FF = 2816
FFN_CONV_WIDTH = 3
ALPHA = (2 * DEPTH) ** 0.25
LN_EPS = 1e-5
GN_EPS = HEAD_B * 1e-5

VMEM_LIMIT_BYTES = 56 * 1024 * 1024
FFN_CHUNK = 256
WKV_CHUNK = 64
WKV_GROUP = 4
WKV_LANES = WKV_GROUP * HEAD_B
SUBLANES = 8
CONV_ROWS = 32


def _params(n_grid):
    return pltpu.CompilerParams(dimension_semantics=("arbitrary",) * n_grid,
                                vmem_limit_bytes=VMEM_LIMIT_BYTES)


def _full(shape):
    nd = len(shape)
    return pl.BlockSpec(shape, lambda *_: (0,) * nd, pipeline_mode=pl.Buffered(1))


def _spec(a):
    if not isinstance(a, tuple):
        return _full(a.shape)
    arr, layer = a
    nd = arr.ndim - 1
    return pl.BlockSpec((None,) + arr.shape[1:], lambda *_: (layer,) + (0,) * nd, pipeline_mode=pl.Buffered(1))


def _arr(a):
    return a[0] if isinstance(a, tuple) else a


def _rows(tm, width):
    return pl.BlockSpec((tm, width), lambda i: (i, 0))


def _dot(a, b):
    return jnp.dot(a.astype(BF16), b.astype(BF16), preferred_element_type=F32)


def _split(a):
    hi = a.astype(BF16)
    lo = (a - hi.astype(F32)).astype(BF16)
    return hi, lo


def _dg(a, b, ca, cb):
    return lax.dot_general(a, b, (((ca,), (cb,)), ((), ())), preferred_element_type=F32)


def _layer_norm(x, g, b, eps):
    xc = x - jnp.mean(x, -1, keepdims=True)
    var = jnp.mean(xc * xc, -1, keepdims=True)
    return xc * lax.rsqrt(var + eps) * g + b


def _ab_in_kernel(x_ref, w_ref, cw_ref, cb_ref, lng_ref, lnb_ref, mu_ref, w0_ref, w2_ref, a0_ref,
                  a2_ref, g2_ref, kkw_ref, kaw_ref, seg_ref, conv0_ref, shift0_ref,
                  ya_ref, r_ref, k_ref, v_ref, ld_ref, kk_ref, b_ref, g_ref, convo_ref, shifto_ref,
                  *rest, bq, tm, x_seq):
    hist = (CONV_A_WIDTH - 1) * bq
    if x_seq:
        xt_ref, ext_scr, sh_scr = rest
        x = jnp.swapaxes(x_ref[...], 0, 1).reshape(tm, D_MODEL)
        xt_ref[...] = x
    else:
        ext_scr, sh_scr = rest
        x = x_ref[...]

    @pl.when(pl.program_id(0) == 0)
    def _():
        ext_scr[0:hist, :] = conv0_ref[...]
        sh_scr[0:bq, :] = shift0_ref[...]

    p = _dot(x, w_ref[...])

    ext_scr[hist:hist + tm, :] = p[:, :D_A] * jax.nn.sigmoid(p[:, D_A:2 * D_A])
    tiles = CONV_ROWS // SUBLANES
    for r0 in range(0, tm, CONV_ROWS):
        acc = jnp.zeros((tiles, SUBLANES, D_A), F32) + cb_ref[...]
        for j in range(CONV_A_WIDTH):
            taps = ext_scr[r0 + j * bq:r0 + j * bq + CONV_ROWS, :].reshape(tiles, SUBLANES, D_A)
            acc = acc + cw_ref[j] * taps
        ya = _layer_norm(acc.reshape(CONV_ROWS, D_A), lng_ref[...], lnb_ref[...], LN_EPS)
        ya_ref[r0:r0 + CONV_ROWS, :] = ya * jax.nn.sigmoid(ya)
    new_hist = ext_scr[tm:tm + hist, :]
    convo_ref[...] = new_hist
    ext_scr[0:hist, :] = new_hist

    pb = p[:, 2 * D_A:]
    sh_scr[bq:bq + tm, :] = pb
    prev = sh_scr[0:tm, :]
    last = pb[tm - bq:, :]
    sh_scr[0:bq, :] = last
    shifto_ref[...] = last
    q = pb + (prev - pb) * mu_ref[...]
    r = q[:, :D_B]
    k = q[:, D_B:2 * D_B]
    v = q[:, 2 * D_B:3 * D_B]
    wa = q[:, 3 * D_B:3 * D_B + LORA_W + LORA_A]
    gl = q[:, 3 * D_B + LORA_W + LORA_A:]
    w = -jax.nn.softplus(-(w0_ref[...] + _dot(jnp.tanh(wa), w2_ref[...]))) - 0.5
    a = jax.nn.sigmoid(a0_ref[...] + _dot(wa, a2_ref[...]))
    kk = k * kkw_ref[...]
    sq_hi, sq_lo = _split(kk * kk)
    ssq = jnp.dot(sq_hi, seg_ref[...], preferred_element_type=F32) + jnp.dot(
        sq_lo, seg_ref[...], preferred_element_type=F32)
    kk = kk * lax.rsqrt(jnp.maximum(ssq, 1e-24))
    def put(ref, val):
        val = val.reshape(tm // bq, bq, D_B)
        pad = ref.shape[1] - tm // bq
        if pad:
            val = jnp.concatenate([val, jnp.zeros((pad, bq, D_B), F32)], axis=0)
        ref[...] = jnp.swapaxes(val, 0, 1)

    g_ref[...] = _dot(jax.nn.sigmoid(gl), g2_ref[...])
    put(r_ref, r)
    put(k_ref, k * (1.0 + (a - 1.0) * kaw_ref[...]))
    put(v_ref, v)
    put(ld_ref, -jnp.exp(w))
    put(kk_ref, kk)
    put(b_ref, kk * a)


def _ab_in(x, conv0, shift0, w_in, cw, cb, lng, lnb, mu, w0, w2p, a0, a2p, g2, kkw, kaw, seg, *, bq, tm, t_pad):
    x_seq = x.ndim == 3
    n = x.shape[0] * x.shape[1] if x_seq else x.shape[0]
    hist = (CONV_A_WIDTH - 1) * bq
    nt = n // tm
    steps = tm // bq
    assert n % tm == 0 and tm % bq == 0 and tm % CONV_ROWS == 0 and (nt == 1 or tm >= hist)
    t_blk = steps if nt > 1 else t_pad
    assert nt * t_blk == t_pad
    seq = jax.ShapeDtypeStruct((bq, t_pad, D_B), F32)
    seq_spec = pl.BlockSpec((bq, t_blk, D_B), lambda i: (0, i, 0))
    ins = [x, w_in, cw, cb, lng, lnb, mu, w0, w2p, a0, a2p, g2, kkw, kaw, seg, conv0, shift0]
    x_spec = pl.BlockSpec((bq, steps, D_MODEL), lambda i: (0, i, 0)) if x_seq else _rows(tm, D_MODEL)
    in_specs = [x_spec] + [_spec(a) for a in ins[1:]]
    extra_specs = [_rows(tm, D_MODEL)] if x_seq else []
    extra_shapes = [jax.ShapeDtypeStruct((n, D_MODEL), F32)] if x_seq else []
    return pl.pallas_call(
        functools.partial(_ab_in_kernel, bq=bq, tm=tm, x_seq=x_seq),
        grid=(nt,),
        in_specs=in_specs,
        out_specs=[_rows(tm, D_A)] + [seq_spec] * 6 + [_rows(tm, D_B), _full((hist, D_A)), _full((bq, D_BP))]
        + extra_specs,
        out_shape=[jax.ShapeDtypeStruct((n, D_A), F32)] + [seq] * 6 + [jax.ShapeDtypeStruct((n, D_B), F32)]
        + [jax.ShapeDtypeStruct((hist, D_A), F32), jax.ShapeDtypeStruct((bq, D_BP), F32)] + extra_shapes,
        scratch_shapes=[pltpu.VMEM((hist + tm, D_A), F32), pltpu.VMEM((bq + tm, D_BP), F32)],
        compiler_params=_params(1), name="ab_in")(*map(_arr, ins))


def _head_blocks(y, width):
    lane_head = lax.broadcasted_iota(jnp.int32, y.shape, 1) // width
    return jnp.concatenate([jnp.where(lane_head == h, y, jnp.zeros_like(y)) for h in range(WKV_GROUP)], axis=0)


def _per_head(x, y, width):
    return _dg(x.astype(BF16), _head_blocks(y.astype(BF16), width), 1, 0)


def _per_head_hp(x, y, width):
    xh, xl = _split(x)
    yh, yl = _split(y)
    m = x.shape[0]
    both = _dg(jnp.concatenate([xh, xl], axis=0), _head_blocks(yh, width), 1, 0)
    return both[:m] + (both[m:] + _dg(xh, _head_blocks(yl, width), 1, 0))


def _wkv_kernel(r_ref, k_ref, v_ref, ld_ref, kk_ref, b_ref, s0_ref, lnxg_ref, lnxb_ref, rk_ref, seg_ref,
                y_ref, so_ref, s_scr, *, c, sb, n_sq):
    n_grp = H_B // WKV_GROUP
    gc = WKV_GROUP * c
    blk_r = lax.broadcasted_iota(jnp.int32, (WKV_LANES, WKV_LANES), 0) // HEAD_B
    blk_c = lax.broadcasted_iota(jnp.int32, (WKV_LANES, WKV_LANES), 1) // HEAD_B
    same_head = blk_r == blk_c

    @pl.when(pl.program_id(1) == 0)
    def _():
        for q in range(sb):
            for gi in range(n_grp):
                s = s0_ref[q, gi * WKV_GROUP:(gi + 1) * WKV_GROUP].reshape(WKV_LANES, HEAD_B)
                s_scr[q, gi] = jnp.where(same_head, jnp.concatenate([s] * WKV_GROUP, axis=1), 0.0)

    row = lax.broadcasted_iota(jnp.int32, (c, c), 0)
    col = lax.broadcasted_iota(jnp.int32, (c, c), 1)
    tri = jnp.where(row >= col, 1.0, 0.0).astype(BF16)
    row_g = lax.broadcasted_iota(jnp.int32, (c, gc), 0)
    col_g = lax.broadcasted_iota(jnp.int32, (c, gc), 1) % c
    incl = row_g >= col_g
    strict = row_g > col_g
    eye = jnp.where(row_g == col_g, 1.0, 0.0).astype(F32)

    left, right, v_g, bk_g, p_end, s_prev = [], [], [], [], [], []
    for q in range(sb):
        ld = ld_ref[q]
        ld_hi = ld.astype(BF16)
        ld_rem = ld - ld_hi.astype(F32)
        ld_mid = ld_rem.astype(BF16)
        ld_lo = (ld_rem - ld_mid.astype(F32)).astype(BF16)
        cum = (jnp.dot(tri, ld_hi, preferred_element_type=F32)
               + (jnp.dot(tri, ld_mid, preferred_element_type=F32)
                  + jnp.dot(tri, ld_lo, preferred_element_type=F32)))
        p_in = jnp.exp(cum)
        p_inv = jnp.exp(-cum)
        kk_t = kk_ref[q] * jnp.exp(cum - ld)
        r_t = r_ref[q] * p_in
        b_t = b_ref[q] * p_inv
        k_t = k_ref[q] * p_inv
        v_all = v_ref[q]
        for gi in range(n_grp):
            gl = slice(gi * WKV_LANES, (gi + 1) * WKV_LANES)
            left.append(jnp.concatenate([kk_t[:, gl], r_t[:, gl]], axis=0).astype(BF16))
            b_bf, k_bf = b_t[:, gl].astype(BF16), k_t[:, gl].astype(BF16)
            right.append(jnp.concatenate([_head_blocks(b_bf, HEAD_B), _head_blocks(k_bf, HEAD_B)], axis=0))
            bk_g.append(jnp.concatenate([b_bf, k_bf], axis=0))
            v_g.append(v_all[:, gl])
            p_end.append(p_in[c - 1:c, gl])
            s_prev.append(s_scr[q, gi])

    chains = range(sb * n_grp)
    gram = [_dg(left[i], right[i], 1, 1) for i in chains]
    a_b = [jnp.where(strict, gram[i][:c, :gc], 0.0) for i in chains]
    a_k = [jnp.where(strict, gram[i][:c, gc:], 0.0) for i in chains]
    a_rb = [jnp.where(incl, gram[i][c:, :gc], 0.0) for i in chains]
    a_rk = [jnp.where(incl, gram[i][c:, gc:], 0.0) for i in chains]
    gs = [_dg(left[i], s_prev[i].astype(BF16), 1, 1) for i in chains]
    av = [_per_head(jnp.concatenate([a_k[i], a_rk[i]], axis=0), v_g[i], HEAD_B) for i in chains]
    rhs = [gs[i][:c] + av[i][:c] for i in chains]
    y_v = [gs[i][c:] + av[i][c:] for i in chains]
    n_pow = [-a_b[i] for i in chains]
    t0 = [eye + n_pow[i] for i in chains]
    if n_sq:
        n_pow = [_per_head(n_pow[i], n_pow[i], c) for i in chains]
        for _ in range(n_sq - 1):
            both = [_per_head(jnp.concatenate([t0[i], n_pow[i]], axis=0), n_pow[i], c) for i in chains]
            t0 = [t0[i] + both[i][:c] for i in chains]
            n_pow = [both[i][c:] for i in chains]
        t0 = [t0[i] + _per_head(t0[i], n_pow[i], c) for i in chains]
    res = [eye - t0[i] - _per_head_hp(a_b[i], t0[i], c) for i in chains]
    z = [rhs[i] + _per_head(res[i], rhs[i], HEAD_B) for i in chains]
    u = [-_per_head_hp(t0[i], z[i], HEAD_B) for i in chains]
    y = [y_v[i] + _per_head(a_rb[i], u[i], HEAD_B) for i in chains]
    upd = [_dg(jnp.concatenate([u[i], v_g[i]], axis=0).astype(BF16), bk_g[i], 0, 0) for i in chains]
    s_new = [(s_prev[i] + jnp.where(same_head, upd[i], 0.0)) * p_end[i] for i in chains]
    for i in chains:
        s_scr[i // n_grp, i % n_grp] = s_new[i]

    rows = sb * c

    def head_sum(x):
        hi, lo = _split(x)
        parts = [hi[:, :WKV_LANES], lo[:, :WKV_LANES], hi[:, WKV_LANES:], lo[:, WKV_LANES:]]
        s = jnp.dot(jnp.concatenate(parts, axis=0), seg_ref[...], preferred_element_type=F32)
        return jnp.concatenate([s[:rows] + s[rows:2 * rows], s[2 * rows:3 * rows] + s[3 * rows:]], axis=1)

    flat = lambda ref: ref[...].reshape(rows, D_B)
    y_all = jnp.concatenate([jnp.concatenate(y[q * n_grp:(q + 1) * n_grp], axis=1) for q in range(sb)], axis=0)
    yc = y_all - head_sum(y_all) * (1.0 / HEAD_B)
    yn = yc * lax.rsqrt(head_sum(yc * yc) * (1.0 / HEAD_B) + GN_EPS)
    bonus = head_sum(flat(r_ref) * flat(k_ref) * rk_ref[...]) * flat(v_ref)
    y_ref[...] = (yn * lnxg_ref[...] + lnxb_ref[...] + bonus).reshape(sb, c, D_B)

    @pl.when(pl.program_id(1) == pl.num_programs(1) - 1)
    def _():
        for q in range(sb):
            for gi in range(n_grp):
                for h in range(WKV_GROUP):
                    hl = slice(h * HEAD_B, (h + 1) * HEAD_B)
                    so_ref[q, gi * WKV_GROUP + h] = s_scr[q, gi, hl, hl]


def _wkv(r, k, v, ld, kk, b, s0, lnxg, lnxb, rk, seg, *, c, sb):
    nb, t, _ = r.shape
    assert t % c == 0 and nb % sb == 0
    n_sq = max(int(math.ceil(math.log2(c))) - 1, 0)
    seq = pl.BlockSpec((sb, c, D_B), lambda i, j: (i, j, 0))
    st = pl.BlockSpec((sb, H_B, HEAD_B, HEAD_B), lambda i, j: (i, 0, 0, 0))
    return pl.pallas_call(
        functools.partial(_wkv_kernel, c=c, sb=sb, n_sq=n_sq),
        grid=(nb // sb, t // c),
        in_specs=[seq] * 6 + [st, _spec(lnxg), _spec(lnxb), _spec(rk), _spec(seg)],
        out_specs=[seq, st],
        out_shape=[jax.ShapeDtypeStruct((nb, t, D_B), F32),
                   jax.ShapeDtypeStruct((nb, H_B, HEAD_B, HEAD_B), F32)],
        scratch_shapes=[pltpu.VMEM((sb, H_B // WKV_GROUP, WKV_LANES, WKV_LANES), F32)],
        compiler_params=_params(2), name="wkv")(r, k, v, ld, kk, b, s0, _arr(lnxg), _arr(lnxb), _arr(rk), seg)


def _ab_out_kernel(x_ref, ya_ref, yb_ref, gate_ref, w_ref, g_ref, b_ref, o_ref, *, bq, tm):
    yb = jnp.swapaxes(yb_ref[...], 0, 1)[:tm // bq].reshape(tm, D_B)
    out = _dot(ya_ref[...], w_ref[0:D_A, :]) + _dot(yb * gate_ref[...], w_ref[D_A:, :])
    o_ref[...] = _layer_norm(ALPHA * x_ref[...] + out, g_ref[...], b_ref[...], LN_EPS)


def _ab_out(x, ya, yb, gate, w_out, g, b, *, bq, tm):
    n = x.shape[0]
    nt = n // tm
    t_blk = tm // bq if nt > 1 else yb.shape[1]
    assert n % tm == 0 and nt * t_blk == yb.shape[1]
    return pl.pallas_call(
        functools.partial(_ab_out_kernel, bq=bq, tm=tm), grid=(nt,),
        in_specs=[_rows(tm, D_MODEL), _rows(tm, D_A), pl.BlockSpec((bq, t_blk, D_B), lambda i: (0, i, 0)),
                  _rows(tm, D_B), _spec(w_out), _spec(g), _spec(b)],
        out_specs=_rows(tm, D_MODEL), out_shape=jax.ShapeDtypeStruct((n, D_MODEL), F32),
        compiler_params=_params(1), name="ab_out")(x, ya, yb, gate, _arr(w_out), _arr(g), _arr(b))


def _s5_coef_kernel(lr_ref, li_ref, ldt_ref, br_ref, bi_ref, are_ref, aim_ref, bbr_ref, bbi_ref):
    lr, li = lr_ref[...], li_ref[...]
    dt = jnp.exp(ldt_ref[...])
    mag = jnp.exp(lr * dt)
    ab_re, ab_im = mag * jnp.cos(li * dt), mag * jnp.sin(li * dt)
    den = lr * lr + li * li
    nr, ni = ab_re - 1.0, ab_im
    f_re, f_im = (nr * lr + ni * li) / den, (ni * lr - nr * li) / den
    are_ref[...] = ab_re
    aim_ref[...] = ab_im
    for ch in range(C_GROUP):
        bbr_ref[ch] = f_re * br_ref[ch] - f_im * bi_ref[ch]
        bbi_ref[ch] = f_re * bi_ref[ch] + f_im * br_ref[ch]


def _s5_coef(lam_re, lam_im, log_dt, b_re, b_im):
    br = jnp.transpose(b_re, (2, 0, 1))
    bi = jnp.transpose(b_im, (2, 0, 1))
    gp = jax.ShapeDtypeStruct((G_C, P_C), F32)
    cgp = jax.ShapeDtypeStruct((C_GROUP, G_C, P_C), F32)
    return pl.pallas_call(_s5_coef_kernel, out_shape=[gp, gp, cgp, cgp], name="s5_coef")(
        lam_re, lam_im, log_dt[:, None], br, bi)


def _gelu_tanh(x):
    return 0.5 * x * (1.0 + jnp.tanh(math.sqrt(2.0 / math.pi) * (x + 0.044715 * (x * x * x))))


def _s5_kernel(x_ref, win_ref, bbd_ref, cre_ref, cim_ref, are_ref, aim_ref, dsk_ref, wout_ref, lng_ref,
               lnb_ref, hre0_ref, him0_ref, o_ref, hreo_ref, himo_ref, bre_scr, bim_scr, hre_scr, him_scr,
               *, bq, tm, lb):
    @pl.when(pl.program_id(0) == 0)
    def _():
        hre_scr[...] = hre0_ref[...]
        him_scr[...] = him0_ref[...]

    x = x_ref[...]
    u = _dot(x, win_ref[...])
    ub = u.astype(BF16)
    ys = []
    for j in range(N_SLAB):
        slab = slice(j * SSM_SLAB, (j + 1) * SSM_SLAB)
        bu = jnp.dot(ub[:, j * SLAB:(j + 1) * SLAB], bbd_ref[j], preferred_element_type=F32)
        bre_scr[:, slab] = bu[:, :SSM_SLAB]
        bim_scr[:, slab] = bu[:, SSM_SLAB:]
        for blk in range(SSM_SLAB // lb):
            cs = slice(j * SSM_SLAB + blk * lb, j * SSM_SLAB + (blk + 1) * lb)
            a_re, a_im = are_ref[:, cs], aim_ref[:, cs]
            h_re, h_im = hre_scr[:, cs], him_scr[:, cs]
            for t in range(tm // bq):
                rows = slice(t * bq, (t + 1) * bq)
                h_re, h_im = (a_re * h_re - a_im * h_im + bre_scr[rows, cs],
                              a_re * h_im + a_im * h_re + bim_scr[rows, cs])
                bre_scr[rows, cs] = h_re
                bim_scr[rows, cs] = h_im
            hre_scr[:, cs] = h_re
            him_scr[:, cs] = h_im
        ys.append(_dot(bre_scr[:, slab], cre_ref[j]) - _dot(bim_scr[:, slab], cim_ref[j]))
    hreo_ref[...] = hre_scr[...]
    himo_ref[...] = him_scr[...]
    y = jnp.concatenate(ys, axis=1) + dsk_ref[...] * u
    o = _dot(_gelu_tanh(y), wout_ref[...])
    out = o[:, :D_MODEL] * jax.nn.sigmoid(o[:, D_MODEL:])
    o_ref[...] = _layer_norm(ALPHA * x + out, lng_ref[...], lnb_ref[...], LN_EPS)


def _s5(x, h_re0, h_im0, w_in, bbd, cre, cim, a_re, a_im, dsk, w_out, lng, lnb, *, bq, tm, lb):
    n = x.shape[0]
    assert n % tm == 0 and tm % bq == 0
    ins = [x, w_in, bbd, cre, cim, a_re, a_im, dsk, w_out, lng, lnb, h_re0, h_im0]
    st = jax.ShapeDtypeStruct((bq, D_SSM), F32)
    return pl.pallas_call(
        functools.partial(_s5_kernel, bq=bq, tm=tm, lb=lb),
        grid=(n // tm,),
        in_specs=[_rows(tm, D_MODEL)] + [_spec(a) for a in ins[1:]],
        out_specs=[_rows(tm, D_MODEL), _full((bq, D_SSM)), _full((bq, D_SSM))],
        out_shape=[jax.ShapeDtypeStruct((n, D_MODEL), F32), st, st],
        scratch_shapes=[pltpu.VMEM((tm, D_SSM), F32), pltpu.VMEM((tm, D_SSM), F32),
                        pltpu.VMEM((bq, D_SSM), F32), pltpu.VMEM((bq, D_SSM), F32)],
        compiler_params=_params(1), name="s5")(*map(_arr, ins))


def _ffn_kernel(x_ref, wup_ref, cw_ref, cb_ref, wdn_ref, lng_ref, lnb_ref, st0_ref, o_ref, sto_ref,
                carry_scr, *, bq, tm, seq_out):
    @pl.when(pl.program_id(0) == 0)
    def _():
        carry_scr[...] = st0_ref[...]

    x = x_ref[...]
    xb = x.astype(BF16)
    acc = jnp.zeros((tm, D_MODEL), F32)
    for ci in range(D_FF // FFN_CHUNK):
        cs = slice(ci * FFN_CHUNK, (ci + 1) * FFN_CHUNK)
        gs = slice(D_FF + ci * FFN_CHUNK, D_FF + (ci + 1) * FFN_CHUNK)
        h = jnp.dot(xb, wup_ref[:, cs], preferred_element_type=F32)
        gate = jnp.dot(xb, wup_ref[:, gs], preferred_element_type=F32)
        old = carry_scr[:, cs]
        h_m1 = jnp.concatenate([old[bq:], h[:tm - bq]], axis=0)
        h_m2 = jnp.concatenate([old, h[:tm - 2 * bq]], axis=0)
        cv = cw_ref[0:1, cs] * h_m2 + cw_ref[1:2, cs] * h_m1 + cw_ref[2:3, cs] * h + cb_ref[:, cs]
        act = cv * jax.nn.sigmoid(cv) * gate
        acc = acc + jnp.dot(act.astype(BF16), wdn_ref[cs, :], preferred_element_type=F32)
        carry_scr[:, cs] = h[tm - 2 * bq:]
    out = _layer_norm(ALPHA * x + acc, lng_ref[...], lnb_ref[...], LN_EPS)
    if seq_out:
        o_ref[...] = jnp.swapaxes(out.reshape(tm // bq, bq, D_MODEL), 0, 1)
    else:
        o_ref[...] = out
    sto_ref[...] = carry_scr[...]


def _ffn(x, st0, w_up, cw, cb, w_dn, lng, lnb, *, bq, tm, seq_out):
    n = x.shape[0]
    assert n % tm == 0 and tm >= 2 * bq and D_FF % FFN_CHUNK == 0
    ins = [x, w_up, cw, cb, w_dn, lng, lnb, st0]
    st = jax.ShapeDtypeStruct(st0.shape, F32)
    if seq_out:
        o_shape = jax.ShapeDtypeStruct((bq, n // bq, D_MODEL), F32)
        o_spec = pl.BlockSpec((bq, tm // bq, D_MODEL), lambda i: (0, i, 0))
    else:
        o_shape, o_spec = jax.ShapeDtypeStruct((n, D_MODEL), F32), _rows(tm, D_MODEL)
    return pl.pallas_call(
        functools.partial(_ffn_kernel, bq=bq, tm=tm, seq_out=seq_out),
        grid=(n // tm,),
        in_specs=[_rows(tm, D_MODEL)] + [_spec(a) for a in ins[1:]],
        out_specs=[o_spec, _full(st0.shape)],
        out_shape=[o_shape, st],
        scratch_shapes=[pltpu.VMEM(st0.shape, F32)],
        compiler_params=_params(1), name="ffn")(*map(_arr, ins))


def _trunk(x, st_conv, st_shift, st_wkv, st_re, st_im, st_ffn, pr, *, bq, t_len, tiles):
    hist = CONV_A_WIDTH - 1
    c = min(WKV_CHUNK, 8 * ((t_len + 7) // 8))
    t_pad = c * ((t_len + c - 1) // c)
    o_conv, o_shift, o_wkv, o_re, o_im, o_ffn = [], [], [], [], [], []
    at = lambda name, layer: (pr[name], layer)

    for i in range(DEPTH):
        j = i // 2
        if i % 2 == 0:
            conv0 = jnp.transpose(st_conv[:, j], (1, 0, 2)).reshape(hist * bq, D_A)
            outs = _ab_in(x, conv0, st_shift[:, j], *(at(n, j) for n in AB_IN_PARAMS), pr["seg"],
                          bq=bq, tm=tiles["ab_in"], t_pad=t_pad)
            ya, r, k, v, ld, kk, bv, gate, conv_n, shift_n = outs[:10]
            if i == 0:
                x = outs[10]
            yb, wkv_n = _wkv(r, k, v, ld, kk, bv, st_wkv[:, j], at("ln_x_g", j), at("ln_x_b", j), at("r_k_b", j),
                             pr["seg_group"], c=c, sb=tiles["wkv_sb"])
            x = _ab_out(x, ya, yb, gate, at("w_out_ab", j), at("ln1_g", i), at("ln1_b", i), bq=bq,
                        tm=tiles["ab_out"])
            o_conv.append(jnp.transpose(conv_n.reshape(hist, bq, D_A), (1, 0, 2)))
            o_shift.append(shift_n)
            o_wkv.append(wkv_n)
        else:
            x, h_re, h_im = _s5(x, st_re[:, j].reshape(bq, D_SSM), st_im[:, j].reshape(bq, D_SSM),
                                *(at(n, j) for n in S5_PARAMS), at("ln1_g", i), at("ln1_b", i),
                                bq=bq, tm=tiles["s5"], lb=tiles["s5_lb"])
            o_re.append(h_re.reshape(bq, G_C, P_C))
            o_im.append(h_im.reshape(bq, G_C, P_C))
        ffn0 = jnp.transpose(st_ffn[:, i], (1, 0, 2)).reshape((FFN_CONV_WIDTH - 1) * bq, D_FF)
        x, ffn_n = _ffn(x, ffn0, *(at(n, i) for n in FFN_PARAMS), bq=bq, tm=tiles["ffn"],
                        seq_out=i == DEPTH - 1)
        o_ffn.append(jnp.transpose(ffn_n.reshape(FFN_CONV_WIDTH - 1, bq, D_FF), (1, 0, 2)))
    stack = lambda xs: jnp.stack(xs, 1)
    return x, stack(o_conv), stack(o_shift), stack(o_wkv), stack(o_re), stack(o_im), stack(o_ffn)


def _prepare(w_in_ab, conv_a_w, conv_a_b, ln_a_g, ln_a_b, mu_b, w0_b, w2_b, a0_b, a2_b, g2_b, k_k_b, k_a_b,
             r_k_b, ln_x_g, ln_x_b, w_out_ab, w_in_c, lam_re, lam_im, log_dt, b_re, b_im, c_re, c_im, d_skip,
             w_out_c, w_up, conv_f_w, conv_f_b, w_down, ln1_g, ln1_b, ln2_g, ln2_b):
    n_ab, n_c = w_in_ab.shape[0], w_in_c.shape[0]
    row = lambda a: a.reshape(a.shape[0], 1, -1)
    zeros = jnp.zeros((n_ab, LORA_W, D_B), F32)
    eye = jnp.eye(G_SLAB, dtype=F32)
    coefs = [_s5_coef(lam_re[j], lam_im[j], log_dt[j], b_re[j], b_im[j]) for j in range(n_c)]

    def drive(bb):
        bb = bb.reshape(C_GROUP, N_SLAB, G_SLAB, P_C)
        return jnp.einsum("csgp,gh->sgchp", bb, eye).reshape(N_SLAB, SLAB, SSM_SLAB)

    def readout(cc):
        cc = cc.reshape(N_SLAB, G_SLAB, C_GROUP, P_C)
        return jnp.einsum("sgcp,gh->shpgc", cc, eye).reshape(N_SLAB, SSM_SLAB, SLAB)

    return {
        "w_in_ab": w_in_ab.astype(BF16), "conv_a_b": row(conv_a_b),
        "conv_a_w": jnp.broadcast_to(conv_a_w[:, :, None, :], (n_ab, CONV_A_WIDTH, SUBLANES, D_A)),
        "ln_a_g": row(ln_a_g), "ln_a_b": row(ln_a_b), "mu_b": row(mu_b), "w0_b": row(w0_b),
        "w2_pad": jnp.concatenate([w2_b, zeros], axis=1).astype(BF16), "a0_b": row(a0_b),
        "a2_pad": jnp.concatenate([zeros, a2_b], axis=1).astype(BF16), "g2_b": g2_b.astype(BF16),
        "k_k_b": row(k_k_b), "k_a_b": row(k_a_b), "r_k_b": row(r_k_b), "ln_x_g": row(ln_x_g),
        "ln_x_b": row(ln_x_b), "w_out_ab": w_out_ab.astype(BF16),
        "seg": jnp.kron(jnp.eye(H_B, dtype=F32), jnp.ones((HEAD_B, HEAD_B), F32)).astype(BF16),
        "seg_group": jnp.kron(jnp.eye(WKV_GROUP, dtype=F32), jnp.ones((HEAD_B, HEAD_B), F32)).astype(BF16),
        "w_in_c": w_in_c.astype(BF16),
        "bbd": jnp.stack([jnp.concatenate([drive(cf[2]), drive(cf[3])], axis=-1) for cf in coefs]).astype(BF16),
        "cre": jnp.stack([readout(c_re[j]) for j in range(n_c)]).astype(BF16),
        "cim": jnp.stack([readout(c_im[j]) for j in range(n_c)]).astype(BF16),
        "a_re": jnp.stack([cf[0].reshape(1, D_SSM) for cf in coefs]),
        "a_im": jnp.stack([cf[1].reshape(1, D_SSM) for cf in coefs]),
        "d_skip": row(d_skip), "w_out_c": w_out_c.astype(BF16),
        "w_up": w_up.astype(BF16), "conv_f_w": conv_f_w, "conv_f_b": row(conv_f_b),
        "w_down": w_down.astype(BF16),
        "ln1_g": row(ln1_g), "ln1_b": row(ln1_b), "ln2_g": row(ln2_g), "ln2_b": row(ln2_b),
    }


AB_IN_PARAMS = ("w_in_ab", "conv_a_w", "conv_a_b", "ln_a_g", "ln_a_b", "mu_b", "w0_b", "w2_pad", "a0_b", "a2_pad",
                "g2_b", "k_k_b", "k_a_b")
S5_PARAMS = ("w_in_c", "bbd", "cre", "cim", "a_re", "a_im", "d_skip", "w_out_c")
FFN_PARAMS = ("w_up", "conv_f_w", "conv_f_b", "w_down", "ln2_g", "ln2_b")
PROMPT_TILES = {"ab_in": 256, "wkv_sb": 8, "ab_out": 512, "s5": 256, "s5_lb": 512, "ffn": 512}
SAMPLE_TILES = {"ab_in": 512, "wkv_sb": 8, "ab_out": 512, "s5": 512, "s5_lb": 128, "ffn": 512}


def _run_group(x, states, pr, tiles):
    bq, t_len, _ = x.shape
    return _trunk(x, *states, pr, bq=bq, t_len=t_len, tiles=tiles)


def kernel(x_prompt, x_sample, state_conv_a, state_shift_b, state_wkv_b, state_ssm_re, state_ssm_im,
           state_conv_ffn, w_in_ab, conv_a_w, conv_a_b, ln_a_g, ln_a_b, mu_b, w0_b, w2_b, a0_b, a2_b, g2_b,
           k_k_b, k_a_b, r_k_b, ln_x_g, ln_x_b, w_out_ab, w_in_c, lam_re, lam_im, log_dt, b_re, b_im, c_re,
           c_im, d_skip, w_out_c, w_up, conv_f_w, conv_f_b, w_down, ln1_g, ln1_b, ln2_g, ln2_b):
    pr = _prepare(w_in_ab, conv_a_w, conv_a_b, ln_a_g, ln_a_b, mu_b, w0_b, w2_b, a0_b, a2_b, g2_b, k_k_b,
                  k_a_b, r_k_b.reshape(r_k_b.shape[0], D_B), ln_x_g, ln_x_b, w_out_ab, w_in_c, lam_re, lam_im,
                  log_dt, b_re, b_im, c_re, c_im, d_skip, w_out_c, w_up, conv_f_w, conv_f_b, w_down, ln1_g,
                  ln1_b, ln2_g, ln2_b)
    sample_states = (state_conv_a, state_shift_b, state_wkv_b, state_ssm_re, state_ssm_im, state_conv_ffn)
    bp = x_prompt.shape[0]
    prompt_states = tuple(jnp.zeros((bp,) + s.shape[1:], s.dtype) for s in sample_states)
    p = _run_group(x_prompt, prompt_states, pr, PROMPT_TILES)
    s = _run_group(x_sample, sample_states, pr, SAMPLE_TILES)
    return (p[0], s[0], p[1], s[1], p[2], s[2], p[3], s[3], p[4], s[4], p[5], s[5], p[6], s[6])
```

```python
import functools
import math

import jax
import jax.numpy as jnp
from jax import lax
from jax.experimental import pallas as pl
from jax.experimental.pallas import tpu as pltpu

F32 = jnp.float32
BF16 = jnp.bfloat16

D_MODEL = 1024
DEPTH = 4
D_A = 512
D_B = 512
HEAD_B = 64
H_B = D_B // HEAD_B
LORA_W = 64
LORA_A = 64
LORA_G = 128
D_BP = 3 * D_B + LORA_W + LORA_A + LORA_G
D_IN_AB = 2 * D_A + D_BP
CONV_A_WIDTH = 31
C_GROUP = 16
G_C = D_MODEL // C_GROUP
P_C = 64
D_SSM = G_C * P_C
SLAB = 128
N_SLAB = D_MODEL // SLAB
G_SLAB = SLAB // C_GROUP
SSM_SLAB = G_SLAB * P_C
D_FF = 2816
FFN_CONV_WIDTH = 3
ALPHA = (2 * DEPTH) ** 0.25
LN_EPS = 1e-5
GN_EPS = HEAD_B * 1e-5

VMEM_LIMIT_BYTES = 56 * 1024 * 1024
FFN_CHUNK = 256
WKV_CHUNK = 64
WKV_GROUP = 4
WKV_LANES = WKV_GROUP * HEAD_B
SUBLANES = 8
CONV_ROWS = 32


def _params(n_grid):
    return pltpu.CompilerParams(dimension_semantics=("arbitrary",) * n_grid,
                                vmem_limit_bytes=VMEM_LIMIT_BYTES)


def _full(shape):
    nd = len(shape)
    return pl.BlockSpec(shape, lambda *_: (0,) * nd, pipeline_mode=pl.Buffered(1))


def _spec(a):
    if not isinstance(a, tuple):
        return _full(a.shape)
    arr, layer = a
    nd = arr.ndim - 1
    return pl.BlockSpec((None,) + arr.shape[1:], lambda *_: (layer,) + (0,) * nd, pipeline_mode=pl.Buffered(1))


def _arr(a):
    return a[0] if isinstance(a, tuple) else a


def _rows(tm, width):
    return pl.BlockSpec((tm, width), lambda i: (i, 0))


def _dot(a, b):
    return jnp.dot(a.astype(BF16), b.astype(BF16), preferred_element_type=F32)


def _split(a):
    hi = a.astype(BF16)
    lo = (a - hi.astype(F32)).astype(BF16)
    return hi, lo


def _dg(a, b, ca, cb):
    return lax.dot_general(a, b, (((ca,), (cb,)), ((), ())), preferred_element_type=F32)


def _layer_norm(x, g, b, eps):
    xc = x - jnp.mean(x, -1, keepdims=True)
    var = jnp.mean(xc * xc, -1, keepdims=True)
    return xc * lax.rsqrt(var + eps) * g + b


def _ab_in_kernel(x_ref, w_ref, cw_ref, cb_ref, lng_ref, lnb_ref, mu_ref, w0_ref, w2_ref, a0_ref,
                  a2_ref, g2_ref, kkw_ref, kaw_ref, seg_ref, conv0_ref, shift0_ref,
                  ya_ref, r_ref, k_ref, v_ref, ld_ref, kk_ref, b_ref, g_ref, convo_ref, shifto_ref,
                  *rest, bq, tm, x_seq):
    hist = (CONV_A_WIDTH - 1) * bq
    if x_seq:
        xt_ref, ext_scr, sh_scr = rest
        x = jnp.swapaxes(x_ref[...], 0, 1).reshape(tm, D_MODEL)
        xt_ref[...] = x
    else:
        ext_scr, sh_scr = rest
        x = x_ref[...]

    @pl.when(pl.program_id(0) == 0)
    def _():
        ext_scr[0:hist, :] = conv0_ref[...]
        sh_scr[0:bq, :] = shift0_ref[...]

    p = _dot(x, w_ref[...])

    ext_scr[hist:hist + tm, :] = p[:, :D_A] * jax.nn.sigmoid(p[:, D_A:2 * D_A])
    tiles = CONV_ROWS // SUBLANES
    for r0 in range(0, tm, CONV_ROWS):
        acc = jnp.zeros((tiles, SUBLANES, D_A), F32) + cb_ref[...]
        for j in range(CONV_A_WIDTH):
            taps = ext_scr[r0 + j * bq:r0 + j * bq + CONV_ROWS, :].reshape(tiles, SUBLANES, D_A)
            acc = acc + cw_ref[j] * taps
        ya = _layer_norm(acc.reshape(CONV_ROWS, D_A), lng_ref[...], lnb_ref[...], LN_EPS)
        ya_ref[r0:r0 + CONV_ROWS, :] = ya * jax.nn.sigmoid(ya)
    new_hist = ext_scr[tm:tm + hist, :]
    convo_ref[...] = new_hist
    ext_scr[0:hist, :] = new_hist

    pb = p[:, 2 * D_A:]
    sh_scr[bq:bq + tm, :] = pb
    prev = sh_scr[0:tm, :]
    last = pb[tm - bq:, :]
    sh_scr[0:bq, :] = last
    shifto_ref[...] = last
    q = pb + (prev - pb) * mu_ref[...]
    r = q[:, :D_B]
    k = q[:, D_B:2 * D_B]
    v = q[:, 2 * D_B:3 * D_B]
    wa = q[:, 3 * D_B:3 * D_B + LORA_W + LORA_A]
    gl = q[:, 3 * D_B + LORA_W + LORA_A:]
    w = -jax.nn.softplus(-(w0_ref[...] + _dot(jnp.tanh(wa), w2_ref[...]))) - 0.5
    a = jax.nn.sigmoid(a0_ref[...] + _dot(wa, a2_ref[...]))
    kk = k * kkw_ref[...]
    sq_hi, sq_lo = _split(kk * kk)
    ssq = jnp.dot(sq_hi, seg_ref[...], preferred_element_type=F32) + jnp.dot(
        sq_lo, seg_ref[...], preferred_element_type=F32)
    kk = kk * lax.rsqrt(jnp.maximum(ssq, 1e-24))
    def put(ref, val):
        val = val.reshape(tm // bq, bq, D_B)
        pad = ref.shape[1] - tm // bq
        if pad:
            val = jnp.concatenate([val, jnp.zeros((pad, bq, D_B), F32)], axis=0)
        ref[...] = jnp.swapaxes(val, 0, 1)

    g_ref[...] = _dot(jax.nn.sigmoid(gl), g2_ref[...])
    put(r_ref, r)
    put(k_ref, k * (1.0 + (a - 1.0) * kaw_ref[...]))
    put(v_ref, v)
    put(ld_ref, -jnp.exp(w))
    put(kk_ref, kk)
    put(b_ref, kk * a)


def _ab_in(x, conv0, shift0, w_in, cw, cb, lng, lnb, mu, w0, w2p, a0, a2p, g2, kkw, kaw, seg, *, bq, tm, t_pad):
    x_seq = x.ndim == 3
    n = x.shape[0] * x.shape[1] if x_seq else x.shape[0]
    hist = (CONV_A_WIDTH - 1) * bq
    nt = n // tm
    steps = tm // bq
    assert n % tm == 0 and tm % bq == 0 and tm % CONV_ROWS == 0 and (nt == 1 or tm >= hist)
    t_blk = steps if nt > 1 else t_pad
    assert nt * t_blk == t_pad
    seq = jax.ShapeDtypeStruct((bq, t_pad, D_B), F32)
    seq_spec = pl.BlockSpec((bq, t_blk, D_B), lambda i: (0, i, 0))
    ins = [x, w_in, cw, cb, lng, lnb, mu, w0, w2p, a0, a2p, g2, kkw, kaw, seg, conv0, shift0]
    x_spec = pl.BlockSpec((bq, steps, D_MODEL), lambda i: (0, i, 0)) if x_seq else _rows(tm, D_MODEL)
    in_specs = [x_spec] + [_spec(a) for a in ins[1:]]
    extra_specs = [_rows(tm, D_MODEL)] if x_seq else []
    extra_shapes = [jax.ShapeDtypeStruct((n, D_MODEL), F32)] if x_seq else []
    return pl.pallas_call(
        functools.partial(_ab_in_kernel, bq=bq, tm=tm, x_seq=x_seq),
        grid=(nt,),
        in_specs=in_specs,
        out_specs=[_rows(tm, D_A)] + [seq_spec] * 6 + [_rows(tm, D_B), _full((hist, D_A)), _full((bq, D_BP))]
        + extra_specs,
        out_shape=[jax.ShapeDtypeStruct((n, D_A), F32)] + [seq] * 6 + [jax.ShapeDtypeStruct((n, D_B), F32)]
        + [jax.ShapeDtypeStruct((hist, D_A), F32), jax.ShapeDtypeStruct((bq, D_BP), F32)] + extra_shapes,
        scratch_shapes=[pltpu.VMEM((hist + tm, D_A), F32), pltpu.VMEM((bq + tm, D_BP), F32)],
        compiler_params=_params(1), name="ab_in")(*map(_arr, ins))


def _head_blocks(y, width):
    lane_head = lax.broadcasted_iota(jnp.int32, y.shape, 1) // width
    return jnp.concatenate([jnp.where(lane_head == h, y, jnp.zeros_like(y)) for h in range(WKV_GROUP)], axis=0)


def _per_head(x, y, width):
    return _dg(x.astype(BF16), _head_blocks(y.astype(BF16), width), 1, 0)


def _per_head_hp(x, y, width):
    xh, xl = _split(x)
    yh, yl = _split(y)
    m = x.shape[0]
    both = _dg(jnp.concatenate([xh, xl], axis=0), _head_blocks(yh, width), 1, 0)
    return both[:m] + (both[m:] + _dg(xh, _head_blocks(yl, width), 1, 0))


def _wkv_kernel(r_ref, k_ref, v_ref, ld_ref, kk_ref, b_ref, s0_ref, lnxg_ref, lnxb_ref, rk_ref, seg_ref,
                y_ref, so_ref, s_scr, *, c, sb, n_sq):
    n_grp = H_B // WKV_GROUP
    gc = WKV_GROUP * c
    blk_r = lax.broadcasted_iota(jnp.int32, (WKV_LANES, WKV_LANES), 0) // HEAD_B
    blk_c = lax.broadcasted_iota(jnp.int32, (WKV_LANES, WKV_LANES), 1) // HEAD_B
    same_head = blk_r == blk_c

    @pl.when(pl.program_id(1) == 0)
    def _():
        for q in range(sb):
            for gi in range(n_grp):
                s = s0_ref[q, gi * WKV_GROUP:(gi + 1) * WKV_GROUP].reshape(WKV_LANES, HEAD_B)
                s_scr[q, gi] = jnp.where(same_head, jnp.concatenate([s] * WKV_GROUP, axis=1), 0.0)

    row = lax.broadcasted_iota(jnp.int32, (c, c), 0)
    col = lax.broadcasted_iota(jnp.int32, (c, c), 1)
    tri = jnp.where(row >= col, 1.0, 0.0).astype(BF16)
    row_g = lax.broadcasted_iota(jnp.int32, (c, gc), 0)
    col_g = lax.broadcasted_iota(jnp.int32, (c, gc), 1) % c
    incl = row_g >= col_g
    strict = row_g > col_g
    eye = jnp.where(row_g == col_g, 1.0, 0.0).astype(F32)

    left, right, v_g, bk_g, p_end, s_prev = [], [], [], [], [], []
    for q in range(sb):
        ld = ld_ref[q]
        ld_hi = ld.astype(BF16)
        ld_rem = ld - ld_hi.astype(F32)
        ld_mid = ld_rem.astype(BF16)
        ld_lo = (ld_rem - ld_mid.astype(F32)).astype(BF16)
        cum = (jnp.dot(tri, ld_hi, preferred_element_type=F32)
               + (jnp.dot(tri, ld_mid, preferred_element_type=F32)
                  + jnp.dot(tri, ld_lo, preferred_element_type=F32)))
        p_in = jnp.exp(cum)
        p_inv = jnp.exp(-cum)
        kk_t = kk_ref[q] * jnp.exp(cum - ld)
        r_t = r_ref[q] * p_in
        b_t = b_ref[q] * p_inv
        k_t = k_ref[q] * p_inv
        v_all = v_ref[q]
        for gi in range(n_grp):
            gl = slice(gi * WKV_LANES, (gi + 1) * WKV_LANES)
            left.append(jnp.concatenate([kk_t[:, gl], r_t[:, gl]], axis=0).astype(BF16))
            b_bf, k_bf = b_t[:, gl].astype(BF16), k_t[:, gl].astype(BF16)
            right.append(jnp.concatenate([_head_blocks(b_bf, HEAD_B), _head_blocks(k_bf, HEAD_B)], axis=0))
            bk_g.append(jnp.concatenate([b_bf, k_bf], axis=0))
            v_g.append(v_all[:, gl])
            p_end.append(p_in[c - 1:c, gl])
            s_prev.append(s_scr[q, gi])

    chains = range(sb * n_grp)
    gram = [_dg(left[i], right[i], 1, 1) for i in chains]
    a_b = [jnp.where(strict, gram[i][:c, :gc], 0.0) for i in chains]
    a_k = [jnp.where(strict, gram[i][:c, gc:], 0.0) for i in chains]
    a_rb = [jnp.where(incl, gram[i][c:, :gc], 0.0) for i in chains]
    a_rk = [jnp.where(incl, gram[i][c:, gc:], 0.0) for i in chains]
    gs = [_dg(left[i], s_prev[i].astype(BF16), 1, 1) for i in chains]
    av = [_per_head(jnp.concatenate([a_k[i], a_rk[i]], axis=0), v_g[i], HEAD_B) for i in chains]
    rhs = [gs[i][:c] + av[i][:c] for i in chains]
    y_v = [gs[i][c:] + av[i][c:] for i in chains]
    n_pow = [-a_b[i] for i in chains]
    t0 = [eye + n_pow[i] for i in chains]
    if n_sq:
        n_pow = [_per_head(n_pow[i], n_pow[i], c) for i in chains]
        for _ in range(n_sq - 1):
            both = [_per_head(jnp.concatenate([t0[i], n_pow[i]], axis=0), n_pow[i], c) for i in chains]
            t0 = [t0[i] + both[i][:c] for i in chains]
            n_pow = [both[i][c:] for i in chains]
        t0 = [t0[i] + _per_head(t0[i], n_pow[i], c) for i in chains]
    res = [eye - t0[i] - _per_head_hp(a_b[i], t0[i], c) for i in chains]
    z = [rhs[i] + _per_head(res[i], rhs[i], HEAD_B) for i in chains]
    u = [-_per_head_hp(t0[i], z[i], HEAD_B) for i in chains]
    y = [y_v[i] + _per_head(a_rb[i], u[i], HEAD_B) for i in chains]
    upd = [_dg(jnp.concatenate([u[i], v_g[i]], axis=0).astype(BF16), bk_g[i], 0, 0) for i in chains]
    s_new = [(s_prev[i] + jnp.where(same_head, upd[i], 0.0)) * p_end[i] for i in chains]
    for i in chains:
        s_scr[i // n_grp, i % n_grp] = s_new[i]

    rows = sb * c

    def head_sum(x):
        hi, lo = _split(x)
        parts = [hi[:, :WKV_LANES], lo[:, :WKV_LANES], hi[:, WKV_LANES:], lo[:, WKV_LANES:]]
        s = jnp.dot(jnp.concatenate(parts, axis=0), seg_ref[...], preferred_element_type=F32)
        return jnp.concatenate([s[:rows] + s[rows:2 * rows], s[2 * rows:3 * rows] + s[3 * rows:]], axis=1)

    flat = lambda ref: ref[...].reshape(rows, D_B)
    y_all = jnp.concatenate([jnp.concatenate(y[q * n_grp:(q + 1) * n_grp], axis=1) for q in range(sb)], axis=0)
    yc = y_all - head_sum(y_all) * (1.0 / HEAD_B)
    yn = yc * lax.rsqrt(head_sum(yc * yc) * (1.0 / HEAD_B) + GN_EPS)
    bonus = head_sum(flat(r_ref) * flat(k_ref) * rk_ref[...]) * flat(v_ref)
    y_ref[...] = (yn * lnxg_ref[...] + lnxb_ref[...] + bonus).reshape(sb, c, D_B)

    @pl.when(pl.program_id(1) == pl.num_programs(1) - 1)
    def _():
        for q in range(sb):
            for gi in range(n_grp):
                for h in range(WKV_GROUP):
                    hl = slice(h * HEAD_B, (h + 1) * HEAD_B)
                    so_ref[q, gi * WKV_GROUP + h] = s_scr[q, gi, hl, hl]


def _wkv(r, k, v, ld, kk, b, s0, lnxg, lnxb, rk, seg, *, c, sb):
    nb, t, _ = r.shape
    assert t % c == 0 and nb % sb == 0
    n_sq = max(int(math.ceil(math.log2(c))) - 1, 0)
    seq = pl.BlockSpec((sb, c, D_B), lambda i, j: (i, j, 0))
    st = pl.BlockSpec((sb, H_B, HEAD_B, HEAD_B), lambda i, j: (i, 0, 0, 0))
    return pl.pallas_call(
        functools.partial(_wkv_kernel, c=c, sb=sb, n_sq=n_sq),
        grid=(nb // sb, t // c),
        in_specs=[seq] * 6 + [st, _spec(lnxg), _spec(lnxb), _spec(rk), _spec(seg)],
        out_specs=[seq, st],
        out_shape=[jax.ShapeDtypeStruct((nb, t, D_B), F32),
                   jax.ShapeDtypeStruct((nb, H_B, HEAD_B, HEAD_B), F32)],
        scratch_shapes=[pltpu.VMEM((sb, H_B // WKV_GROUP, WKV_LANES, WKV_LANES), F32)],
        compiler_params=_params(2), name="wkv")(r, k, v, ld, kk, b, s0, _arr(lnxg), _arr(lnxb), _arr(rk), seg)


def _ab_out_kernel(x_ref, ya_ref, yb_ref, gate_ref, w_ref, g_ref, b_ref, o_ref, *, bq, tm):
    yb = jnp.swapaxes(yb_ref[...], 0, 1)[:tm // bq].reshape(tm, D_B)
    out = _dot(ya_ref[...], w_ref[0:D_A, :]) + _dot(yb * gate_ref[...], w_ref[D_A:, :])
    o_ref[...] = _layer_norm(ALPHA * x_ref[...] + out, g_ref[...], b_ref[...], LN_EPS)


def _ab_out(x, ya, yb, gate, w_out, g, b, *, bq, tm):
    n = x.shape[0]
    nt = n // tm
    t_blk = tm // bq if nt > 1 else yb.shape[1]
    assert n % tm == 0 and nt * t_blk == yb.shape[1]
    return pl.pallas_call(
        functools.partial(_ab_out_kernel, bq=bq, tm=tm), grid=(nt,),
        in_specs=[_rows(tm, D_MODEL), _rows(tm, D_A), pl.BlockSpec((bq, t_blk, D_B), lambda i: (0, i, 0)),
                  _rows(tm, D_B), _spec(w_out), _spec(g), _spec(b)],
        out_specs=_rows(tm, D_MODEL), out_shape=jax.ShapeDtypeStruct((n, D_MODEL), F32),
        compiler_params=_params(1), name="ab_out")(x, ya, yb, gate, _arr(w_out), _arr(g), _arr(b))


def _s5_coef_kernel(lr_ref, li_ref, ldt_ref, br_ref, bi_ref, are_ref, aim_ref, bbr_ref, bbi_ref):
    lr, li = lr_ref[...], li_ref[...]
    dt = jnp.exp(ldt_ref[...])
    mag = jnp.exp(lr * dt)
    ab_re, ab_im = mag * jnp.cos(li * dt), mag * jnp.sin(li * dt)
    den = lr * lr + li * li
    nr, ni = ab_re - 1.0, ab_im
    f_re, f_im = (nr * lr + ni * li) / den, (ni * lr - nr * li) / den
    are_ref[...] = ab_re
    aim_ref[...] = ab_im
    for ch in range(C_GROUP):
        bbr_ref[ch] = f_re * br_ref[ch] - f_im * bi_ref[ch]
        bbi_ref[ch] = f_re * bi_ref[ch] + f_im * br_ref[ch]


def _s5_coef(lam_re, lam_im, log_dt, b_re, b_im):
    br = jnp.transpose(b_re, (2, 0, 1))
    bi = jnp.transpose(b_im, (2, 0, 1))
    gp = jax.ShapeDtypeStruct((G_C, P_C), F32)
    cgp = jax.ShapeDtypeStruct((C_GROUP, G_C, P_C), F32)
    return pl.pallas_call(_s5_coef_kernel, out_shape=[gp, gp, cgp, cgp], name="s5_coef")(
        lam_re, lam_im, log_dt[:, None], br, bi)


def _gelu_tanh(x):
    return 0.5 * x * (1.0 + jnp.tanh(math.sqrt(2.0 / math.pi) * (x + 0.044715 * (x * x * x))))


def _s5_kernel(x_ref, win_ref, bbd_ref, cre_ref, cim_ref, are_ref, aim_ref, dsk_ref, wout_ref, lng_ref,
               lnb_ref, hre0_ref, him0_ref, o_ref, hreo_ref, himo_ref, bre_scr, bim_scr, hre_scr, him_scr,
               *, bq, tm, lb):
    @pl.when(pl.program_id(0) == 0)
    def _():
        hre_scr[...] = hre0_ref[...]
        him_scr[...] = him0_ref[...]

    x = x_ref[...]
    u = _dot(x, win_ref[...])
    ub = u.astype(BF16)
    ys = []
    for j in range(N_SLAB):
        slab = slice(j * SSM_SLAB, (j + 1) * SSM_SLAB)
        bu = jnp.dot(ub[:, j * SLAB:(j + 1) * SLAB], bbd_ref[j], preferred_element_type=F32)
        bre_scr[:, slab] = bu[:, :SSM_SLAB]
        bim_scr[:, slab] = bu[:, SSM_SLAB:]
        for blk in range(SSM_SLAB // lb):
            cs = slice(j * SSM_SLAB + blk * lb, j * SSM_SLAB + (blk + 1) * lb)
            a_re, a_im = are_ref[:, cs], aim_ref[:, cs]
            h_re, h_im = hre_scr[:, cs], him_scr[:, cs]
            for t in range(tm // bq):
                rows = slice(t * bq, (t + 1) * bq)
                h_re, h_im = (a_re * h_re - a_im * h_im + bre_scr[rows, cs],
                              a_re * h_im + a_im * h_re + bim_scr[rows, cs])
                bre_scr[rows, cs] = h_re
                bim_scr[rows, cs] = h_im
            hre_scr[:, cs] = h_re
            him_scr[:, cs] = h_im
        ys.append(_dot(bre_scr[:, slab], cre_ref[j]) - _dot(bim_scr[:, slab], cim_ref[j]))
    hreo_ref[...] = hre_scr[...]
    himo_ref[...] = him_scr[...]
    y = jnp.concatenate(ys, axis=1) + dsk_ref[...] * u
    o = _dot(_gelu_tanh(y), wout_ref[...])
    out = o[:, :D_MODEL] * jax.nn.sigmoid(o[:, D_MODEL:])
    o_ref[...] = _layer_norm(ALPHA * x + out, lng_ref[...], lnb_ref[...], LN_EPS)


def _s5(x, h_re0, h_im0, w_in, bbd, cre, cim, a_re, a_im, dsk, w_out, lng, lnb, *, bq, tm, lb):
    n = x.shape[0]
    assert n % tm == 0 and tm % bq == 0
    ins = [x, w_in, bbd, cre, cim, a_re, a_im, dsk, w_out, lng, lnb, h_re0, h_im0]
    st = jax.ShapeDtypeStruct((bq, D_SSM), F32)
    return pl.pallas_call(
        functools.partial(_s5_kernel, bq=bq, tm=tm, lb=lb),
        grid=(n // tm,),
        in_specs=[_rows(tm, D_MODEL)] + [_spec(a) for a in ins[1:]],
        out_specs=[_rows(tm, D_MODEL), _full((bq, D_SSM)), _full((bq, D_SSM))],
        out_shape=[jax.ShapeDtypeStruct((n, D_MODEL), F32), st, st],
        scratch_shapes=[pltpu.VMEM((tm, D_SSM), F32), pltpu.VMEM((tm, D_SSM), F32),
                        pltpu.VMEM((bq, D_SSM), F32), pltpu.VMEM((bq, D_SSM), F32)],
        compiler_params=_params(1), name="s5")(*map(_arr, ins))


def _ffn_kernel(x_ref, wup_ref, cw_ref, cb_ref, wdn_ref, lng_ref, lnb_ref, st0_ref, o_ref, sto_ref,
                carry_scr, *, bq, tm, seq_out):
    @pl.when(pl.program_id(0) == 0)
    def _():
        carry_scr[...] = st0_ref[...]

    x = x_ref[...]
    xb = x.astype(BF16)
    acc = jnp.zeros((tm, D_MODEL), F32)
    for ci in range(D_FF // FFN_CHUNK):
        cs = slice(ci * FFN_CHUNK, (ci + 1) * FFN_CHUNK)
        gs = slice(D_FF + ci * FFN_CHUNK, D_FF + (ci + 1) * FFN_CHUNK)
        h = jnp.dot(xb, wup_ref[:, cs], preferred_element_type=F32)
        gate = jnp.dot(xb, wup_ref[:, gs], preferred_element_type=F32)
        old = carry_scr[:, cs]
        h_m1 = jnp.concatenate([old[bq:], h[:tm - bq]], axis=0)
        h_m2 = jnp.concatenate([old, h[:tm - 2 * bq]], axis=0)
        cv = cw_ref[0:1, cs] * h_m2 + cw_ref[1:2, cs] * h_m1 + cw_ref[2:3, cs] * h + cb_ref[:, cs]
        act = cv * jax.nn.sigmoid(cv) * gate
        acc = acc + jnp.dot(act.astype(BF16), wdn_ref[cs, :], preferred_element_type=F32)
        carry_scr[:, cs] = h[tm - 2 * bq:]
    out = _layer_norm(ALPHA * x + acc, lng_ref[...], lnb_ref[...], LN_EPS)
    if seq_out:
        o_ref[...] = jnp.swapaxes(out.reshape(tm // bq, bq, D_MODEL), 0, 1)
    else:
        o_ref[...] = out
    sto_ref[...] = carry_scr[...]


def _ffn(x, st0, w_up, cw, cb, w_dn, lng, lnb, *, bq, tm, seq_out):
    n = x.shape[0]
    assert n % tm == 0 and tm >= 2 * bq and D_FF % FFN_CHUNK == 0
    ins = [x, w_up, cw, cb, w_dn, lng, lnb, st0]
    st = jax.ShapeDtypeStruct(st0.shape, F32)
    if seq_out:
        o_shape = jax.ShapeDtypeStruct((bq, n // bq, D_MODEL), F32)
        o_spec = pl.BlockSpec((bq, tm // bq, D_MODEL), lambda i: (0, i, 0))
    else:
        o_shape, o_spec = jax.ShapeDtypeStruct((n, D_MODEL), F32), _rows(tm, D_MODEL)
    return pl.pallas_call(
        functools.partial(_ffn_kernel, bq=bq, tm=tm, seq_out=seq_out),
        grid=(n // tm,),
        in_specs=[_rows(tm, D_MODEL)] + [_spec(a) for a in ins[1:]],
        out_specs=[o_spec, _full(st0.shape)],
        out_shape=[o_shape, st],
        scratch_shapes=[pltpu.VMEM(st0.shape, F32)],
        compiler_params=_params(1), name="ffn")(*map(_arr, ins))


def _trunk(x, st_conv, st_shift, st_wkv, st_re, st_im, st_ffn, pr, *, bq, t_len, tiles):
    hist = CONV_A_WIDTH - 1
    c = min(WKV_CHUNK, 8 * ((t_len + 7) // 8))
    t_pad = c * ((t_len + c - 1) // c)
    o_conv, o_shift, o_wkv, o_re, o_im, o_ffn = [], [], [], [], [], []
    at = lambda name, layer: (pr[name], layer)

    for i in range(DEPTH):
        j = i // 2
        if i % 2 == 0:
            conv0 = jnp.transpose(st_conv[:, j], (1, 0, 2)).reshape(hist * bq, D_A)
            outs = _ab_in(x, conv0, st_shift[:, j], *(at(n, j) for n in AB_IN_PARAMS), pr["seg"],
                          bq=bq, tm=tiles["ab_in"], t_pad=t_pad)
            ya, r, k, v, ld, kk, bv, gate, conv_n, shift_n = outs[:10]
            if i == 0:
                x = outs[10]
            yb, wkv_n = _wkv(r, k, v, ld, kk, bv, st_wkv[:, j], at("ln_x_g", j), at("ln_x_b", j), at("r_k_b", j),
                             pr["seg_group"], c=c, sb=tiles["wkv_sb"])
            x = _ab_out(x, ya, yb, gate, at("w_out_ab", j), at("ln1_g", i), at("ln1_b", i), bq=bq,
                        tm=tiles["ab_out"])
            o_conv.append(jnp.transpose(conv_n.reshape(hist, bq, D_A), (1, 0, 2)))
            o_shift.append(shift_n)
            o_wkv.append(wkv_n)
        else:
            x, h_re, h_im = _s5(x, st_re[:, j].reshape(bq, D_SSM), st_im[:, j].reshape(bq, D_SSM),
                                *(at(n, j) for n in S5_PARAMS), at("ln1_g", i), at("ln1_b", i),
                                bq=bq, tm=tiles["s5"], lb=tiles["s5_lb"])
            o_re.append(h_re.reshape(bq, G_C, P_C))
            o_im.append(h_im.reshape(bq, G_C, P_C))
        ffn0 = jnp.transpose(st_ffn[:, i], (1, 0, 2)).reshape((FFN_CONV_WIDTH - 1) * bq, D_FF)
        x, ffn_n = _ffn(x, ffn0, *(at(n, i) for n in FFN_PARAMS), bq=bq, tm=tiles["ffn"],
                        seq_out=i == DEPTH - 1)
        o_ffn.append(jnp.transpose(ffn_n.reshape(FFN_CONV_WIDTH - 1, bq, D_FF), (1, 0, 2)))
    stack = lambda xs: jnp.stack(xs, 1)
    return x, stack(o_conv), stack(o_shift), stack(o_wkv), stack(o_re), stack(o_im), stack(o_ffn)


def _prepare(w_in_ab, conv_a_w, conv_a_b, ln_a_g, ln_a_b, mu_b, w0_b, w2_b, a0_b, a2_b, g2_b, k_k_b, k_a_b,
             r_k_b, ln_x_g, ln_x_b, w_out_ab, w_in_c, lam_re, lam_im, log_dt, b_re, b_im, c_re, c_im, d_skip,
             w_out_c, w_up, conv_f_w, conv_f_b, w_down, ln1_g, ln1_b, ln2_g, ln2_b):
    n_ab, n_c = w_in_ab.shape[0], w_in_c.shape[0]
    row = lambda a: a.reshape(a.shape[0], 1, -1)
    zeros = jnp.zeros((n_ab, LORA_W, D_B), F32)
    eye = jnp.eye(G_SLAB, dtype=F32)
    coefs = [_s5_coef(lam_re[j], lam_im[j], log_dt[j], b_re[j], b_im[j]) for j in range(n_c)]

    def drive(bb):
        bb = bb.reshape(C_GROUP, N_SLAB, G_SLAB, P_C)
        return jnp.einsum("csgp,gh->sgchp", bb, eye).reshape(N_SLAB, SLAB, SSM_SLAB)

    def readout(cc):
        cc = cc.reshape(N_SLAB, G_SLAB, C_GROUP, P_C)
        return jnp.einsum("sgcp,gh->shpgc", cc, eye).reshape(N_SLAB, SSM_SLAB, SLAB)

    return {
        "w_in_ab": w_in_ab.astype(BF16), "conv_a_b": row(conv_a_b),
        "conv_a_w": jnp.broadcast_to(conv_a_w[:, :, None, :], (n_ab, CONV_A_WIDTH, SUBLANES, D_A)),
        "ln_a_g": row(ln_a_g), "ln_a_b": row(ln_a_b), "mu_b": row(mu_b), "w0_b": row(w0_b),
        "w2_pad": jnp.concatenate([w2_b, zeros], axis=1).astype(BF16), "a0_b": row(a0_b),
        "a2_pad": jnp.concatenate([zeros, a2_b], axis=1).astype(BF16), "g2_b": g2_b.astype(BF16),
        "k_k_b": row(k_k_b), "k_a_b": row(k_a_b), "r_k_b": row(r_k_b), "ln_x_g": row(ln_x_g),
        "ln_x_b": row(ln_x_b), "w_out_ab": w_out_ab.astype(BF16),
        "seg": jnp.kron(jnp.eye(H_B, dtype=F32), jnp.ones((HEAD_B, HEAD_B), F32)).astype(BF16),
        "seg_group": jnp.kron(jnp.eye(WKV_GROUP, dtype=F32), jnp.ones((HEAD_B, HEAD_B), F32)).astype(BF16),
        "w_in_c": w_in_c.astype(BF16),
        "bbd": jnp.stack([jnp.concatenate([drive(cf[2]), drive(cf[3])], axis=-1) for cf in coefs]).astype(BF16),
        "cre": jnp.stack([readout(c_re[j]) for j in range(n_c)]).astype(BF16),
        "cim": jnp.stack([readout(c_im[j]) for j in range(n_c)]).astype(BF16),
        "a_re": jnp.stack([cf[0].reshape(1, D_SSM) for cf in coefs]),
        "a_im": jnp.stack([cf[1].reshape(1, D_SSM) for cf in coefs]),
        "d_skip": row(d_skip), "w_out_c": w_out_c.astype(BF16),
        "w_up": w_up.astype(BF16), "conv_f_w": conv_f_w, "conv_f_b": row(conv_f_b),
        "w_down": w_down.astype(BF16),
        "ln1_g": row(ln1_g), "ln1_b": row(ln1_b), "ln2_g": row(ln2_g), "ln2_b": row(ln2_b),
    }


AB_IN_PARAMS = ("w_in_ab", "conv_a_w", "conv_a_b", "ln_a_g", "ln_a_b", "mu_b", "w0_b", "w2_pad", "a0_b", "a2_pad",
                "g2_b", "k_k_b", "k_a_b")
S5_PARAMS = ("w_in_c", "bbd", "cre", "cim", "a_re", "a_im", "d_skip", "w_out_c")
FFN_PARAMS = ("w_up", "conv_f_w", "conv_f_b", "w_down", "ln2_g", "ln2_b")
PROMPT_TILES = {"ab_in": 256, "wkv_sb": 8, "ab_out": 512, "s5": 512, "s5_lb": 512, "ffn": 512}
SAMPLE_TILES = {"ab_in": 512, "wkv_sb": 8, "ab_out": 512, "s5": 512, "s5_lb": 128, "ffn": 512}


def _run_group(x, states, pr, tiles):
    bq, t_len, _ = x.shape
    return _trunk(x, *states, pr, bq=bq, t_len=t_len, tiles=tiles)


def kernel(x_prompt, x_sample, state_conv_a, state_shift_b, state_wkv_b, state_ssm_re, state_ssm_im,
           state_conv_ffn, w_in_ab, conv_a_w, conv_a_b, ln_a_g, ln_a_b, mu_b, w0_b, w2_b, a0_b, a2_b, g2_b,
           k_k_b, k_a_b, r_k_b, ln_x_g, ln_x_b, w_out_ab, w_in_c, lam_re, lam_im, log_dt, b_re, b_im, c_re,
           c_im, d_skip, w_out_c, w_up, conv_f_w, conv_f_b, w_down, ln1_g, ln1_b, ln2_g, ln2_b):
    pr = _prepare(w_in_ab, conv_a_w, conv_a_b, ln_a_g, ln_a_b, mu_b, w0_b, w2_b, a0_b, a2_b, g2_b, k_k_b,
                  k_a_b, r_k_b.reshape(r_k_b.shape[0], D_B), ln_x_g, ln_x_b, w_out_ab, w_in_c, lam_re, lam_im,
                  log_dt, b_re, b_im, c_re, c_im, d_skip, w_out_c, w_up, conv_f_w, conv_f_b, w_down, ln1_g,
                  ln1_b, ln2_g, ln2_b)
    sample_states = (state_conv_a, state_shift_b, state_wkv_b, state_ssm_re, state_ssm_im, state_conv_ffn)
    bp = x_prompt.shape[0]
    prompt_states = tuple(jnp.zeros((bp,) + s.shape[1:], s.dtype) for s in sample_states)
    p = _run_group(x_prompt, prompt_states, pr, PROMPT_TILES)
    s = _run_group(x_sample, sample_states, pr, SAMPLE_TILES)
    return (p[0], s[0], p[1], s[1], p[2], s[2], p[3], s[3], p[4], s[4], p[5], s[5], p[6], s[6])
```

```python
import functools
import math

import jax
import jax.numpy as jnp
from jax import lax
from jax.experimental import pallas as pl
from jax.experimental.pallas import tpu as pltpu

F32 = jnp.float32
BF16 = jnp.bfloat16

D_MODEL = 1024
DEPTH = 4
D_A = 512
D_B = 512
HEAD_B = 64
H_B = D_B // HEAD_B
LORA_W = 64
LORA_A = 64
LORA_G = 128
D_BP = 3 * D_B + LORA_W + LORA_A + LORA_G
D_IN_AB = 2 * D_A + D_BP
CONV_A_WIDTH = 31
C_GROUP = 16
G_C = D_MODEL // C_GROUP
P_C = 64
D_SSM = G_C * P_C
SLAB = 128
N_SLAB = D_MODEL // SLAB
G_SLAB = SLAB // C_GROUP
SSM_SLAB = G_SLAB * P_C
D_FF = 2816
FFN_CONV_WIDTH = 3
ALPHA = (2 * DEPTH) ** 0.25
LN_EPS = 1e-5
GN_EPS = HEAD_B * 1e-5

VMEM_LIMIT_BYTES = 56 * 1024 * 1024
FFN_CHUNK = 256
WKV_CHUNK = 64
WKV_GROUP = 4
WKV_LANES = WKV_GROUP * HEAD_B
SUBLANES = 8
CONV_ROWS = 32


def _params(n_grid):
    return pltpu.CompilerParams(dimension_semantics=("arbitrary",) * n_grid,
                                vmem_limit_bytes=VMEM_LIMIT_BYTES)


def _full(shape):
    nd = len(shape)
    return pl.BlockSpec(shape, lambda *_: (0,) * nd, pipeline_mode=pl.Buffered(1))


def _spec(a):
    if not isinstance(a, tuple):
        return _full(a.shape)
    arr, layer, *axis = a
    ax = axis[0] if axis else 0
    index = (0,) * ax + (layer,) + (0,) * (arr.ndim - ax - 1)
    return pl.BlockSpec(arr.shape[:ax] + (None,) + arr.shape[ax + 1:], lambda *_: index,
                        pipeline_mode=pl.Buffered(1))


def _arr(a):
    return a[0] if isinstance(a, tuple) else a


def _rows(tm, width):
    return pl.BlockSpec((tm, width), lambda i: (i, 0))


def _dot(a, b):
    return jnp.dot(a.astype(BF16), b.astype(BF16), preferred_element_type=F32)


def _split(a):
    hi = a.astype(BF16)
    lo = (a - hi.astype(F32)).astype(BF16)
    return hi, lo


def _dg(a, b, ca, cb):
    return lax.dot_general(a, b, (((ca,), (cb,)), ((), ())), preferred_element_type=F32)


def _layer_norm(x, g, b, eps):
    xc = x - jnp.mean(x, -1, keepdims=True)
    var = jnp.mean(xc * xc, -1, keepdims=True)
    return xc * lax.rsqrt(var + eps) * g + b


def _ab_in_kernel(x_ref, w_ref, cw_ref, cb_ref, lng_ref, lnb_ref, mu_ref, w0_ref, w2_ref, a0_ref,
                  a2_ref, g2_ref, kkw_ref, kaw_ref, seg_ref, conv0_ref, shift0_ref,
                  ya_ref, r_ref, k_ref, v_ref, ld_ref, kk_ref, b_ref, g_ref, convo_ref, shifto_ref,
                  *rest, bq, tm, x_seq):
    hist = (CONV_A_WIDTH - 1) * bq
    if x_seq:
        xt_ref, ext_scr, sh_scr = rest
        x = jnp.swapaxes(x_ref[...], 0, 1).reshape(tm, D_MODEL)
        xt_ref[...] = x
    else:
        ext_scr, sh_scr = rest
        x = x_ref[...]

    @pl.when(pl.program_id(0) == 0)
    def _():
        ext_scr[0:hist, :] = jnp.swapaxes(conv0_ref[...], 0, 1).reshape(hist, D_A)
        sh_scr[0:bq, :] = shift0_ref[...]

    p = _dot(x, w_ref[...])

    ext_scr[hist:hist + tm, :] = p[:, :D_A] * jax.nn.sigmoid(p[:, D_A:2 * D_A])
    tiles = CONV_ROWS // SUBLANES
    for r0 in range(0, tm, CONV_ROWS):
        acc = jnp.zeros((tiles, SUBLANES, D_A), F32) + cb_ref[...]
        for j in range(CONV_A_WIDTH):
            taps = ext_scr[r0 + j * bq:r0 + j * bq + CONV_ROWS, :].reshape(tiles, SUBLANES, D_A)
            acc = acc + cw_ref[j] * taps
        ya = _layer_norm(acc.reshape(CONV_ROWS, D_A), lng_ref[...], lnb_ref[...], LN_EPS)
        ya_ref[r0:r0 + CONV_ROWS, :] = ya * jax.nn.sigmoid(ya)
    new_hist = ext_scr[tm:tm + hist, :]
    ext_scr[0:hist, :] = new_hist

    pb = p[:, 2 * D_A:]
    sh_scr[bq:bq + tm, :] = pb
    prev = sh_scr[0:tm, :]
    last = pb[tm - bq:, :]
    sh_scr[0:bq, :] = last
    shifto_ref[...] = last
    q = pb + (prev - pb) * mu_ref[...]
    r = q[:, :D_B]
    k = q[:, D_B:2 * D_B]
    v = q[:, 2 * D_B:3 * D_B]
    wa = q[:, 3 * D_B:3 * D_B + LORA_W + LORA_A]
    gl = q[:, 3 * D_B + LORA_W + LORA_A:]
    w = -jax.nn.softplus(-(w0_ref[...] + _dot(jnp.tanh(wa), w2_ref[...]))) - 0.5
    a = jax.nn.sigmoid(a0_ref[...] + _dot(wa, a2_ref[...]))
    kk = k * kkw_ref[...]
    sq_hi, sq_lo = _split(kk * kk)
    ssq = jnp.dot(sq_hi, seg_ref[...], preferred_element_type=F32) + jnp.dot(
        sq_lo, seg_ref[...], preferred_element_type=F32)
    kk = kk * lax.rsqrt(jnp.maximum(ssq, 1e-24))
    def put(ref, val):
        val = val.reshape(tm // bq, bq, D_B)
        pad = ref.shape[1] - tm // bq
        if pad:
            val = jnp.concatenate([val, jnp.zeros((pad, bq, D_B), F32)], axis=0)
        ref[...] = jnp.swapaxes(val, 0, 1)

    g_ref[...] = _dot(jax.nn.sigmoid(gl), g2_ref[...])
    put(r_ref, r)
    put(k_ref, k * (1.0 + (a - 1.0) * kaw_ref[...]))
    put(v_ref, v)
    put(ld_ref, -jnp.exp(w))
    put(kk_ref, kk)
    put(b_ref, kk * a)

    @pl.when(pl.program_id(0) == pl.num_programs(0) - 1)
    def _():
        convo_ref[...] = jnp.swapaxes(ext_scr[0:hist, :].reshape(CONV_A_WIDTH - 1, bq, D_A), 0, 1)


def _ab_in(x, conv0, shift0, w_in, cw, cb, lng, lnb, mu, w0, w2p, a0, a2p, g2, kkw, kaw, seg, *, bq, tm, t_pad):
    x_seq = x.ndim == 3
    n = x.shape[0] * x.shape[1] if x_seq else x.shape[0]
    hist = (CONV_A_WIDTH - 1) * bq
    nt = n // tm
    steps = tm // bq
    assert n % tm == 0 and tm % bq == 0 and tm % CONV_ROWS == 0 and (nt == 1 or tm >= hist)
    t_blk = steps if nt > 1 else t_pad
    assert nt * t_blk == t_pad
    seq = jax.ShapeDtypeStruct((bq, t_pad, D_B), F32)
    seq_spec = pl.BlockSpec((bq, t_blk, D_B), lambda i: (0, i, 0))
    ins = [x, w_in, cw, cb, lng, lnb, mu, w0, w2p, a0, a2p, g2, kkw, kaw, seg, conv0, shift0]
    x_spec = pl.BlockSpec((bq, steps, D_MODEL), lambda i: (0, i, 0)) if x_seq else _rows(tm, D_MODEL)
    in_specs = [x_spec] + [_spec(a) for a in ins[1:]]
    extra_specs = [_rows(tm, D_MODEL)] if x_seq else []
    extra_shapes = [jax.ShapeDtypeStruct((n, D_MODEL), F32)] if x_seq else []
    return pl.pallas_call(
        functools.partial(_ab_in_kernel, bq=bq, tm=tm, x_seq=x_seq),
        grid=(nt,),
        in_specs=in_specs,
        out_specs=[_rows(tm, D_A)] + [seq_spec] * 6
        + [_rows(tm, D_B), _full((bq, CONV_A_WIDTH - 1, D_A)), _full((bq, D_BP))] + extra_specs,
        out_shape=[jax.ShapeDtypeStruct((n, D_A), F32)] + [seq] * 6 + [jax.ShapeDtypeStruct((n, D_B), F32)]
        + [jax.ShapeDtypeStruct((bq, CONV_A_WIDTH - 1, D_A), F32), jax.ShapeDtypeStruct((bq, D_BP), F32)]
        + extra_shapes,
        scratch_shapes=[pltpu.VMEM((hist + tm, D_A), F32), pltpu.VMEM((bq + tm, D_BP), F32)],
        compiler_params=_params(1), name="ab_in")(*map(_arr, ins))


def _head_blocks(y, width):
    lane_head = lax.broadcasted_iota(jnp.int32, y.shape, 1) // width
    return jnp.concatenate([jnp.where(lane_head == h, y, jnp.zeros_like(y)) for h in range(WKV_GROUP)], axis=0)


def _per_head(x, y, width):
    return _dg(x.astype(BF16), _head_blocks(y.astype(BF16), width), 1, 0)


def _per_head_hp(x, y, width):
    xh, xl = _split(x)
    yh, yl = _split(y)
    m = x.shape[0]
    both = _dg(jnp.concatenate([xh, xl], axis=0), _head_blocks(yh, width), 1, 0)
    return both[:m] + (both[m:] + _dg(xh, _head_blocks(yl, width), 1, 0))


def _wkv_kernel(r_ref, k_ref, v_ref, ld_ref, kk_ref, b_ref, s0_ref, lnxg_ref, lnxb_ref, rk_ref, seg_ref,
                y_ref, so_ref, s_scr, *, c, sb, n_sq):
    n_grp = H_B // WKV_GROUP
    gc = WKV_GROUP * c
    blk_r = lax.broadcasted_iota(jnp.int32, (WKV_LANES, WKV_LANES), 0) // HEAD_B
    blk_c = lax.broadcasted_iota(jnp.int32, (WKV_LANES, WKV_LANES), 1) // HEAD_B
    same_head = blk_r == blk_c

    @pl.when(pl.program_id(1) == 0)
    def _():
        for q in range(sb):
            for gi in range(n_grp):
                s = s0_ref[q, gi * WKV_GROUP:(gi + 1) * WKV_GROUP].reshape(WKV_LANES, HEAD_B)
                s_scr[q, gi] = jnp.where(same_head, jnp.concatenate([s] * WKV_GROUP, axis=1), 0.0)

    row = lax.broadcasted_iota(jnp.int32, (c, c), 0)
    col = lax.broadcasted_iota(jnp.int32, (c, c), 1)
    tri = jnp.where(row >= col, 1.0, 0.0).astype(BF16)
    row_g = lax.broadcasted_iota(jnp.int32, (c, gc), 0)
    col_g = lax.broadcasted_iota(jnp.int32, (c, gc), 1) % c
    incl = row_g >= col_g
    strict = row_g > col_g
    eye = jnp.where(row_g == col_g, 1.0, 0.0).astype(F32)

    left, right, v_g, bk_g, p_end, s_prev = [], [], [], [], [], []
    for q in range(sb):
        ld = ld_ref[q]
        ld_hi = ld.astype(BF16)
        ld_rem = ld - ld_hi.astype(F32)
        ld_mid = ld_rem.astype(BF16)
        ld_lo = (ld_rem - ld_mid.astype(F32)).astype(BF16)
        cum = (jnp.dot(tri, ld_hi, preferred_element_type=F32)
               + (jnp.dot(tri, ld_mid, preferred_element_type=F32)
                  + jnp.dot(tri, ld_lo, preferred_element_type=F32)))
        p_in = jnp.exp(cum)
        p_inv = jnp.exp(-cum)
        kk_t = kk_ref[q] * jnp.exp(cum - ld)
        r_t = r_ref[q] * p_in
        b_t = b_ref[q] * p_inv
        k_t = k_ref[q] * p_inv
        v_all = v_ref[q]
        for gi in range(n_grp):
            gl = slice(gi * WKV_LANES, (gi + 1) * WKV_LANES)
            left.append(jnp.concatenate([kk_t[:, gl], r_t[:, gl]], axis=0).astype(BF16))
            b_bf, k_bf = b_t[:, gl].astype(BF16), k_t[:, gl].astype(BF16)
            right.append(jnp.concatenate([_head_blocks(b_bf, HEAD_B), _head_blocks(k_bf, HEAD_B)], axis=0))
            bk_g.append(jnp.concatenate([b_bf, k_bf], axis=0))
            v_g.append(v_all[:, gl])
            p_end.append(p_in[c - 1:c, gl])
            s_prev.append(s_scr[q, gi])

    chains = range(sb * n_grp)
    gram = [_dg(left[i], right[i], 1, 1) for i in chains]
    a_b = [jnp.where(strict, gram[i][:c, :gc], 0.0) for i in chains]
    a_k = [jnp.where(strict, gram[i][:c, gc:], 0.0) for i in chains]
    a_rb = [jnp.where(incl, gram[i][c:, :gc], 0.0) for i in chains]
    a_rk = [jnp.where(incl, gram[i][c:, gc:], 0.0) for i in chains]
    gs = [_dg(left[i], s_prev[i].astype(BF16), 1, 1) for i in chains]
    av = [_per_head(jnp.concatenate([a_k[i], a_rk[i]], axis=0), v_g[i], HEAD_B) for i in chains]
    rhs = [gs[i][:c] + av[i][:c] for i in chains]
    y_v = [gs[i][c:] + av[i][c:] for i in chains]
    n_pow = [-a_b[i] for i in chains]
    t0 = [eye + n_pow[i] for i in chains]
    if n_sq:
        n_pow = [_per_head(n_pow[i], n_pow[i], c) for i in chains]
        for _ in range(n_sq - 1):
            both = [_per_head(jnp.concatenate([t0[i], n_pow[i]], axis=0), n_pow[i], c) for i in chains]
            t0 = [t0[i] + both[i][:c] for i in chains]
            n_pow = [both[i][c:] for i in chains]
        t0 = [t0[i] + _per_head(t0[i], n_pow[i], c) for i in chains]
    res = [eye - t0[i] - _per_head_hp(a_b[i], t0[i], c) for i in chains]
    z = [rhs[i] + _per_head(res[i], rhs[i], HEAD_B) for i in chains]
    u = [-_per_head_hp(t0[i], z[i], HEAD_B) for i in chains]
    y = [y_v[i] + _per_head(a_rb[i], u[i], HEAD_B) for i in chains]
    upd = [_dg(jnp.concatenate([u[i], v_g[i]], axis=0).astype(BF16), bk_g[i], 0, 0) for i in chains]
    s_new = [(s_prev[i] + jnp.where(same_head, upd[i], 0.0)) * p_end[i] for i in chains]
    for i in chains:
        s_scr[i // n_grp, i % n_grp] = s_new[i]

    rows = sb * c

    def head_sum(x):
        hi, lo = _split(x)
        parts = [hi[:, :WKV_LANES], lo[:, :WKV_LANES], hi[:, WKV_LANES:], lo[:, WKV_LANES:]]
        s = jnp.dot(jnp.concatenate(parts, axis=0), seg_ref[...], preferred_element_type=F32)
        return jnp.concatenate([s[:rows] + s[rows:2 * rows], s[2 * rows:3 * rows] + s[3 * rows:]], axis=1)

    flat = lambda ref: ref[...].reshape(rows, D_B)
    y_all = jnp.concatenate([jnp.concatenate(y[q * n_grp:(q + 1) * n_grp], axis=1) for q in range(sb)], axis=0)
    yc = y_all - head_sum(y_all) * (1.0 / HEAD_B)
    yn = yc * lax.rsqrt(head_sum(yc * yc) * (1.0 / HEAD_B) + GN_EPS)
    bonus = head_sum(flat(r_ref) * flat(k_ref) * rk_ref[...]) * flat(v_ref)
    y_ref[...] = (yn * lnxg_ref[...] + lnxb_ref[...] + bonus).reshape(sb, c, D_B)

    @pl.when(pl.program_id(1) == pl.num_programs(1) - 1)
    def _():
        for q in range(sb):
            for gi in range(n_grp):
                for h in range(WKV_GROUP):
                    hl = slice(h * HEAD_B, (h + 1) * HEAD_B)
                    so_ref[q, gi * WKV_GROUP + h] = s_scr[q, gi, hl, hl]


def _wkv(r, k, v, ld, kk, b, s0, lnxg, lnxb, rk, seg, *, c, sb):
    nb, t, _ = r.shape
    assert t % c == 0 and nb % sb == 0
    n_sq = max(int(math.ceil(math.log2(c))) - 1, 0)
    seq = pl.BlockSpec((sb, c, D_B), lambda i, j: (i, j, 0))
    st = pl.BlockSpec((sb, H_B, HEAD_B, HEAD_B), lambda i, j: (i, 0, 0, 0))
    return pl.pallas_call(
        functools.partial(_wkv_kernel, c=c, sb=sb, n_sq=n_sq),
        grid=(nb // sb, t // c),
        in_specs=[seq] * 6 + [st, _spec(lnxg), _spec(lnxb), _spec(rk), _spec(seg)],
        out_specs=[seq, st],
        out_shape=[jax.ShapeDtypeStruct((nb, t, D_B), F32),
                   jax.ShapeDtypeStruct((nb, H_B, HEAD_B, HEAD_B), F32)],
        scratch_shapes=[pltpu.VMEM((sb, H_B // WKV_GROUP, WKV_LANES, WKV_LANES), F32)],
        compiler_params=_params(2), name="wkv")(r, k, v, ld, kk, b, s0, _arr(lnxg), _arr(lnxb), _arr(rk), seg)


def _ab_out_kernel(x_ref, ya_ref, yb_ref, gate_ref, w_ref, g_ref, b_ref, o_ref, *, bq, tm):
    yb = jnp.swapaxes(yb_ref[...], 0, 1)[:tm // bq].reshape(tm, D_B)
    out = _dot(ya_ref[...], w_ref[0:D_A, :]) + _dot(yb * gate_ref[...], w_ref[D_A:, :])
    o_ref[...] = _layer_norm(ALPHA * x_ref[...] + out, g_ref[...], b_ref[...], LN_EPS)


def _ab_out(x, ya, yb, gate, w_out, g, b, *, bq, tm):
    n = x.shape[0]
    nt = n // tm
    t_blk = tm // bq if nt > 1 else yb.shape[1]
    assert n % tm == 0 and nt * t_blk == yb.shape[1]
    return pl.pallas_call(
        functools.partial(_ab_out_kernel, bq=bq, tm=tm), grid=(nt,),
        in_specs=[_rows(tm, D_MODEL), _rows(tm, D_A), pl.BlockSpec((bq, t_blk, D_B), lambda i: (0, i, 0)),
                  _rows(tm, D_B), _spec(w_out), _spec(g), _spec(b)],
        out_specs=_rows(tm, D_MODEL), out_shape=jax.ShapeDtypeStruct((n, D_MODEL), F32),
        compiler_params=_params(1), name="ab_out")(x, ya, yb, gate, _arr(w_out), _arr(g), _arr(b))


def _s5_coef_kernel(lr_ref, li_ref, ldt_ref, br_ref, bi_ref, are_ref, aim_ref, bbr_ref, bbi_ref):
    lr, li = lr_ref[...], li_ref[...]
    dt = jnp.exp(ldt_ref[...])
    mag = jnp.exp(lr * dt)
    ab_re, ab_im = mag * jnp.cos(li * dt), mag * jnp.sin(li * dt)
    den = lr * lr + li * li
    nr, ni = ab_re - 1.0, ab_im
    f_re, f_im = (nr * lr + ni * li) / den, (ni * lr - nr * li) / den
    are_ref[...] = ab_re
    aim_ref[...] = ab_im
    for ch in range(C_GROUP):
        bbr_ref[ch] = f_re * br_ref[ch] - f_im * bi_ref[ch]
        bbi_ref[ch] = f_re * bi_ref[ch] + f_im * br_ref[ch]


def _s5_coef(lam_re, lam_im, log_dt, b_re, b_im):
    br = jnp.transpose(b_re, (2, 0, 1))
    bi = jnp.transpose(b_im, (2, 0, 1))
    gp = jax.ShapeDtypeStruct((G_C, P_C), F32)
    cgp = jax.ShapeDtypeStruct((C_GROUP, G_C, P_C), F32)
    return pl.pallas_call(_s5_coef_kernel, out_shape=[gp, gp, cgp, cgp], name="s5_coef")(
        lam_re, lam_im, log_dt[:, None], br, bi)


def _gelu_tanh(x):
    return 0.5 * x * (1.0 + jnp.tanh(math.sqrt(2.0 / math.pi) * (x + 0.044715 * (x * x * x))))


def _s5_kernel(x_ref, win_ref, bbd_ref, cre_ref, cim_ref, are_ref, aim_ref, dsk_ref, wout_ref, lng_ref,
               lnb_ref, hre0_ref, him0_ref, o_ref, hreo_ref, himo_ref, bre_scr, bim_scr, hre_scr, him_scr,
               *, bq, tm, lb):
    @pl.when(pl.program_id(0) == 0)
    def _():
        hre_scr[...] = hre0_ref[...]
        him_scr[...] = him0_ref[...]

    x = x_ref[...]
    u = _dot(x, win_ref[...])
    ub = u.astype(BF16)
    ys = []
    for j in range(N_SLAB):
        slab = slice(j * SSM_SLAB, (j + 1) * SSM_SLAB)
        bu = jnp.dot(ub[:, j * SLAB:(j + 1) * SLAB], bbd_ref[j], preferred_element_type=F32)
        bre_scr[:, slab] = bu[:, :SSM_SLAB]
        bim_scr[:, slab] = bu[:, SSM_SLAB:]
        for blk in range(SSM_SLAB // lb):
            cs = slice(j * SSM_SLAB + blk * lb, j * SSM_SLAB + (blk + 1) * lb)
            a_re, a_im = are_ref[:, cs], aim_ref[:, cs]
            h_re, h_im = hre_scr[:, cs], him_scr[:, cs]
            for t in range(tm // bq):
                rows = slice(t * bq, (t + 1) * bq)
                h_re, h_im = (a_re * h_re - a_im * h_im + bre_scr[rows, cs],
                              a_re * h_im + a_im * h_re + bim_scr[rows, cs])
                bre_scr[rows, cs] = h_re
                bim_scr[rows, cs] = h_im
            hre_scr[:, cs] = h_re
            him_scr[:, cs] = h_im
        ys.append(_dot(bre_scr[:, slab], cre_ref[j]) - _dot(bim_scr[:, slab], cim_ref[j]))
    hreo_ref[...] = hre_scr[...]
    himo_ref[...] = him_scr[...]
    y = jnp.concatenate(ys, axis=1) + dsk_ref[...] * u
    o = _dot(_gelu_tanh(y), wout_ref[...])
    out = o[:, :D_MODEL] * jax.nn.sigmoid(o[:, D_MODEL:])
    o_ref[...] = _layer_norm(ALPHA * x + out, lng_ref[...], lnb_ref[...], LN_EPS)


def _s5(x, h_re0, h_im0, w_in, bbd, cre, cim, a_re, a_im, dsk, w_out, lng, lnb, *, bq, tm, lb):
    n = x.shape[0]
    assert n % tm == 0 and tm % bq == 0
    ins = [x, w_in, bbd, cre, cim, a_re, a_im, dsk, w_out, lng, lnb, h_re0, h_im0]
    st = jax.ShapeDtypeStruct((bq, D_SSM), F32)
    return pl.pallas_call(
        functools.partial(_s5_kernel, bq=bq, tm=tm, lb=lb),
        grid=(n // tm,),
        in_specs=[_rows(tm, D_MODEL)] + [_spec(a) for a in ins[1:]],
        out_specs=[_rows(tm, D_MODEL), _full((bq, D_SSM)), _full((bq, D_SSM))],
        out_shape=[jax.ShapeDtypeStruct((n, D_MODEL), F32), st, st],
        scratch_shapes=[pltpu.VMEM((tm, D_SSM), F32), pltpu.VMEM((tm, D_SSM), F32),
                        pltpu.VMEM((bq, D_SSM), F32), pltpu.VMEM((bq, D_SSM), F32)],
        compiler_params=_params(1), name="s5")(*map(_arr, ins))


def _ffn_kernel(x_ref, wup_ref, cw_ref, cb_ref, wdn_ref, lng_ref, lnb_ref, st0_ref, o_ref, sto_ref,
                carry_scr, *, bq, tm, seq_out):
    @pl.when(pl.program_id(0) == 0)
    def _():
        carry_scr[...] = st0_ref[...]

    x = x_ref[...]
    xb = x.astype(BF16)
    acc = jnp.zeros((tm, D_MODEL), F32)
    for ci in range(D_FF // FFN_CHUNK):
        cs = slice(ci * FFN_CHUNK, (ci + 1) * FFN_CHUNK)
        gs = slice(D_FF + ci * FFN_CHUNK, D_FF + (ci + 1) * FFN_CHUNK)
        h = jnp.dot(xb, wup_ref[:, cs], preferred_element_type=F32)
        gate = jnp.dot(xb, wup_ref[:, gs], preferred_element_type=F32)
        old = carry_scr[:, cs]
        h_m1 = jnp.concatenate([old[bq:], h[:tm - bq]], axis=0)
        h_m2 = jnp.concatenate([old, h[:tm - 2 * bq]], axis=0)
        cv = cw_ref[0:1, cs] * h_m2 + cw_ref[1:2, cs] * h_m1 + cw_ref[2:3, cs] * h + cb_ref[:, cs]
        act = cv * jax.nn.sigmoid(cv) * gate
        acc = acc + jnp.dot(act.astype(BF16), wdn_ref[cs, :], preferred_element_type=F32)
        carry_scr[:, cs] = h[tm - 2 * bq:]
    out = _layer_norm(ALPHA * x + acc, lng_ref[...], lnb_ref[...], LN_EPS)
    if seq_out:
        o_ref[...] = jnp.swapaxes(out.reshape(tm // bq, bq, D_MODEL), 0, 1)
    else:
        o_ref[...] = out
    sto_ref[...] = carry_scr[...]


def _ffn(x, st0, w_up, cw, cb, w_dn, lng, lnb, *, bq, tm, seq_out):
    n = x.shape[0]
    assert n % tm == 0 and tm >= 2 * bq and D_FF % FFN_CHUNK == 0
    ins = [x, w_up, cw, cb, w_dn, lng, lnb, st0]
    st = jax.ShapeDtypeStruct(st0.shape, F32)
    if seq_out:
        o_shape = jax.ShapeDtypeStruct((bq, n // bq, D_MODEL), F32)
        o_spec = pl.BlockSpec((bq, tm // bq, D_MODEL), lambda i: (0, i, 0))
    else:
        o_shape, o_spec = jax.ShapeDtypeStruct((n, D_MODEL), F32), _rows(tm, D_MODEL)
    return pl.pallas_call(
        functools.partial(_ffn_kernel, bq=bq, tm=tm, seq_out=seq_out),
        grid=(n // tm,),
        in_specs=[_rows(tm, D_MODEL)] + [_spec(a) for a in ins[1:]],
        out_specs=[o_spec, _full(st0.shape)],
        out_shape=[o_shape, st],
        scratch_shapes=[pltpu.VMEM(st0.shape, F32)],
        compiler_params=_params(1), name="ffn")(*map(_arr, ins))


def _trunk(x, st_conv, st_shift, st_wkv, st_re, st_im, st_ffn, pr, *, bq, t_len, tiles):
    hist = CONV_A_WIDTH - 1
    c = min(WKV_CHUNK, 8 * ((t_len + 7) // 8))
    t_pad = c * ((t_len + c - 1) // c)
    o_conv, o_shift, o_wkv, o_re, o_im, o_ffn = [], [], [], [], [], []
    at = lambda name, layer: (pr[name], layer)

    for i in range(DEPTH):
        j = i // 2
        if i % 2 == 0:
            outs = _ab_in(x, (st_conv, j, 1), st_shift[:, j], *(at(n, j) for n in AB_IN_PARAMS), pr["seg"],
                          bq=bq, tm=tiles["ab_in"], t_pad=t_pad)
            ya, r, k, v, ld, kk, bv, gate, conv_n, shift_n = outs[:10]
            if i == 0:
                x = outs[10]
            yb, wkv_n = _wkv(r, k, v, ld, kk, bv, st_wkv[:, j], at("ln_x_g", j), at("ln_x_b", j), at("r_k_b", j),
                             pr["seg_group"], c=c, sb=tiles["wkv_sb"])
            x = _ab_out(x, ya, yb, gate, at("w_out_ab", j), at("ln1_g", i), at("ln1_b", i), bq=bq,
                        tm=tiles["ab_out"])
            o_conv.append(conv_n)
            o_shift.append(shift_n)
            o_wkv.append(wkv_n)
        else:
            x, h_re, h_im = _s5(x, st_re[:, j].reshape(bq, D_SSM), st_im[:, j].reshape(bq, D_SSM),
                                *(at(n, j) for n in S5_PARAMS), at("ln1_g", i), at("ln1_b", i),
                                bq=bq, tm=tiles["s5"], lb=tiles["s5_lb"])
            o_re.append(h_re.reshape(bq, G_C, P_C))
            o_im.append(h_im.reshape(bq, G_C, P_C))
        ffn0 = jnp.transpose(st_ffn[:, i], (1, 0, 2)).reshape((FFN_CONV_WIDTH - 1) * bq, D_FF)
        x, ffn_n = _ffn(x, ffn0, *(at(n, i) for n in FFN_PARAMS), bq=bq, tm=tiles["ffn"],
                        seq_out=i == DEPTH - 1)
        o_ffn.append(jnp.transpose(ffn_n.reshape(FFN_CONV_WIDTH - 1, bq, D_FF), (1, 0, 2)))
    stack = lambda xs: jnp.stack(xs, 1)
    return x, stack(o_conv), stack(o_shift), stack(o_wkv), stack(o_re), stack(o_im), stack(o_ffn)


def _prepare(w_in_ab, conv_a_w, conv_a_b, ln_a_g, ln_a_b, mu_b, w0_b, w2_b, a0_b, a2_b, g2_b, k_k_b, k_a_b,
             r_k_b, ln_x_g, ln_x_b, w_out_ab, w_in_c, lam_re, lam_im, log_dt, b_re, b_im, c_re, c_im, d_skip,
             w_out_c, w_up, conv_f_w, conv_f_b, w_down, ln1_g, ln1_b, ln2_g, ln2_b):
    n_ab, n_c = w_in_ab.shape[0], w_in_c.shape[0]
    row = lambda a: a.reshape(a.shape[0], 1, -1)
    zeros = jnp.zeros((n_ab, LORA_W, D_B), F32)
    eye = jnp.eye(G_SLAB, dtype=F32)
    coefs = [_s5_coef(lam_re[j], lam_im[j], log_dt[j], b_re[j], b_im[j]) for j in range(n_c)]

    def drive(bb):
        bb = bb.reshape(C_GROUP, N_SLAB, G_SLAB, P_C)
        return jnp.einsum("csgp,gh->sgchp", bb, eye).reshape(N_SLAB, SLAB, SSM_SLAB)

    def readout(cc):
        cc = cc.reshape(N_SLAB, G_SLAB, C_GROUP, P_C)
        return jnp.einsum("sgcp,gh->shpgc", cc, eye).reshape(N_SLAB, SSM_SLAB, SLAB)

    return {
        "w_in_ab": w_in_ab.astype(BF16), "conv_a_b": row(conv_a_b),
        "conv_a_w": jnp.broadcast_to(conv_a_w[:, :, None, :], (n_ab, CONV_A_WIDTH, SUBLANES, D_A)),
        "ln_a_g": row(ln_a_g), "ln_a_b": row(ln_a_b), "mu_b": row(mu_b), "w0_b": row(w0_b),
        "w2_pad": jnp.concatenate([w2_b, zeros], axis=1).astype(BF16), "a0_b": row(a0_b),
        "a2_pad": jnp.concatenate([zeros, a2_b], axis=1).astype(BF16), "g2_b": g2_b.astype(BF16),
        "k_k_b": row(k_k_b), "k_a_b": row(k_a_b), "r_k_b": row(r_k_b), "ln_x_g": row(ln_x_g),
        "ln_x_b": row(ln_x_b), "w_out_ab": w_out_ab.astype(BF16),
        "seg": jnp.kron(jnp.eye(H_B, dtype=F32), jnp.ones((HEAD_B, HEAD_B), F32)).astype(BF16),
        "seg_group": jnp.kron(jnp.eye(WKV_GROUP, dtype=F32), jnp.ones((HEAD_B, HEAD_B), F32)).astype(BF16),
        "w_in_c": w_in_c.astype(BF16),
        "bbd": jnp.stack([jnp.concatenate([drive(cf[2]), drive(cf[3])], axis=-1) for cf in coefs]).astype(BF16),
        "cre": jnp.stack([readout(c_re[j]) for j in range(n_c)]).astype(BF16),
        "cim": jnp.stack([readout(c_im[j]) for j in range(n_c)]).astype(BF16),
        "a_re": jnp.stack([cf[0].reshape(1, D_SSM) for cf in coefs]),
        "a_im": jnp.stack([cf[1].reshape(1, D_SSM) for cf in coefs]),
        "d_skip": row(d_skip), "w_out_c": w_out_c.astype(BF16),
        "w_up": w_up.astype(BF16), "conv_f_w": conv_f_w, "conv_f_b": row(conv_f_b),
        "w_down": w_down.astype(BF16),
        "ln1_g": row(ln1_g), "ln1_b": row(ln1_b), "ln2_g": row(ln2_g), "ln2_b": row(ln2_b),
    }


AB_IN_PARAMS = ("w_in_ab", "conv_a_w", "conv_a_b", "ln_a_g", "ln_a_b", "mu_b", "w0_b", "w2_pad", "a0_b", "a2_pad",
                "g2_b", "k_k_b", "k_a_b")
S5_PARAMS = ("w_in_c", "bbd", "cre", "cim", "a_re", "a_im", "d_skip", "w_out_c")
FFN_PARAMS = ("w_up", "conv_f_w", "conv_f_b", "w_down", "ln2_g", "ln2_b")
PROMPT_TILES = {"ab_in": 256, "wkv_sb": 8, "ab_out": 512, "s5": 512, "s5_lb": 512, "ffn": 512}
SAMPLE_TILES = {"ab_in": 512, "wkv_sb": 8, "ab_out": 512, "s5": 512, "s5_lb": 128, "ffn": 512}


def _run_group(x, states, pr, tiles):
    bq, t_len, _ = x.shape
    return _trunk(x, *states, pr, bq=bq, t_len=t_len, tiles=tiles)


def kernel(x_prompt, x_sample, state_conv_a, state_shift_b, state_wkv_b, state_ssm_re, state_ssm_im,
           state_conv_ffn, w_in_ab, conv_a_w, conv_a_b, ln_a_g, ln_a_b, mu_b, w0_b, w2_b, a0_b, a2_b, g2_b,
           k_k_b, k_a_b, r_k_b, ln_x_g, ln_x_b, w_out_ab, w_in_c, lam_re, lam_im, log_dt, b_re, b_im, c_re,
           c_im, d_skip, w_out_c, w_up, conv_f_w, conv_f_b, w_down, ln1_g, ln1_b, ln2_g, ln2_b):
    pr = _prepare(w_in_ab, conv_a_w, conv_a_b, ln_a_g, ln_a_b, mu_b, w0_b, w2_b, a0_b, a2_b, g2_b, k_k_b,
                  k_a_b, r_k_b.reshape(r_k_b.shape[0], D_B), ln_x_g, ln_x_b, w_out_ab, w_in_c, lam_re, lam_im,
                  log_dt, b_re, b_im, c_re, c_im, d_skip, w_out_c, w_up, conv_f_w, conv_f_b, w_down, ln1_g,
                  ln1_b, ln2_g, ln2_b)
    sample_states = (state_conv_a, state_shift_b, state_wkv_b, state_ssm_re, state_ssm_im, state_conv_ffn)
    bp = x_prompt.shape[0]
    prompt_states = tuple(jnp.zeros((bp,) + s.shape[1:], s.dtype) for s in sample_states)
    p = _run_group(x_prompt, prompt_states, pr, PROMPT_TILES)
    s = _run_group(x_sample, sample_states, pr, SAMPLE_TILES)
    return (p[0], s[0], p[1], s[1], p[2], s[2], p[3], s[3], p[4], s[4], p[5], s[5], p[6], s[6])
```

```python
import functools
import math

import jax
import jax.numpy as jnp
from jax import lax
from jax.experimental import pallas as pl
from jax.experimental.pallas import tpu as pltpu

F32 = jnp.float32
BF16 = jnp.bfloat16

D_MODEL = 1024
DEPTH = 4
D_A = 512
D_B = 512
HEAD_B = 64
H_B = D_B // HEAD_B
LORA_W = 64
LORA_A = 64
LORA_G = 128
D_BP = 3 * D_B + LORA_W + LORA_A + LORA_G
D_IN_AB = 2 * D_A + D_BP
CONV_A_WIDTH = 31
C_GROUP = 16
G_C = D_MODEL // C_GROUP
P_C = 64
D_SSM = G_C * P_C
SLAB = 128
N_SLAB = D_MODEL // SLAB
G_SLAB = SLAB // C_GROUP
SSM_SLAB = G_SLAB * P_C
D_FF = 2816
FFN_CONV_WIDTH = 3
ALPHA = (2 * DEPTH) ** 0.25
LN_EPS = 1e-5
GN_EPS = HEAD_B * 1e-5

VMEM_LIMIT_BYTES = 56 * 1024 * 1024
FFN_CHUNK = 256
WKV_CHUNK = 64
WKV_GROUP = 4
WKV_LANES = WKV_GROUP * HEAD_B
SUBLANES = 8
CONV_IN_FLIGHT = 2
CONV_ROWS = 32


def _params(n_grid):
    return pltpu.CompilerParams(dimension_semantics=("arbitrary",) * n_grid,
                                vmem_limit_bytes=VMEM_LIMIT_BYTES)


def _full(shape):
    nd = len(shape)
    return pl.BlockSpec(shape, lambda *_: (0,) * nd, pipeline_mode=pl.Buffered(1))


def _spec(a):
    if not isinstance(a, tuple):
        return _full(a.shape)
    arr, layer, *axis = a
    ax = axis[0] if axis else 0
    index = (0,) * ax + (layer,) + (0,) * (arr.ndim - ax - 1)
    return pl.BlockSpec(arr.shape[:ax] + (None,) + arr.shape[ax + 1:], lambda *_: index,
                        pipeline_mode=pl.Buffered(1))


def _arr(a):
    return a[0] if isinstance(a, tuple) else a


def _rows(tm, width):
    return pl.BlockSpec((tm, width), lambda i: (i, 0))


def _dot(a, b):
    return jnp.dot(a.astype(BF16), b.astype(BF16), preferred_element_type=F32)


def _split(a):
    hi = a.astype(BF16)
    lo = (a - hi.astype(F32)).astype(BF16)
    return hi, lo


def _dg(a, b, ca, cb):
    return lax.dot_general(a, b, (((ca,), (cb,)), ((), ())), preferred_element_type=F32)


def _layer_norm(x, g, b, eps):
    xc = x - jnp.mean(x, -1, keepdims=True)
    var = jnp.mean(xc * xc, -1, keepdims=True)
    return xc * lax.rsqrt(var + eps) * g + b


def _ab_in_kernel(x_ref, w_ref, cw_ref, cb_ref, lng_ref, lnb_ref, mu_ref, w0_ref, w2_ref, a0_ref,
                  a2_ref, g2_ref, kkw_ref, kaw_ref, seg_ref, conv0_ref, shift0_ref,
                  ya_ref, r_ref, k_ref, v_ref, ld_ref, kk_ref, b_ref, g_ref, convo_ref, shifto_ref,
                  *rest, bq, tm, x_seq):
    hist = (CONV_A_WIDTH - 1) * bq
    if x_seq:
        xt_ref, ext_scr, sh_scr = rest
        x = jnp.swapaxes(x_ref[...], 0, 1).reshape(tm, D_MODEL)
        xt_ref[...] = x
    else:
        ext_scr, sh_scr = rest
        x = x_ref[...]

    @pl.when(pl.program_id(0) == 0)
    def _():
        ext_scr[0:hist, :] = jnp.swapaxes(conv0_ref[...], 0, 1).reshape(hist, D_A)
        sh_scr[0:bq, :] = shift0_ref[...]

    p = _dot(x, w_ref[...])

    ext_scr[hist:hist + tm, :] = p[:, :D_A] * jax.nn.sigmoid(p[:, D_A:2 * D_A])
    tiles = CONV_ROWS // SUBLANES
    done = []
    for r0 in range(0, tm, CONV_ROWS):
        acc = jnp.zeros((tiles, SUBLANES, D_A), F32) + cb_ref[...]
        if len(done) >= CONV_IN_FLIGHT:
            tail = done[-CONV_IN_FLIGHT][0:SUBLANES, :]
            acc = acc + jnp.where(tail != tail, tail, 0.0)
        for j in range(CONV_A_WIDTH):
            taps = ext_scr[r0 + j * bq:r0 + j * bq + CONV_ROWS, :].reshape(tiles, SUBLANES, D_A)
            acc = acc + cw_ref[j] * taps
        ya = _layer_norm(acc.reshape(CONV_ROWS, D_A), lng_ref[...], lnb_ref[...], LN_EPS)
        ya = ya * jax.nn.sigmoid(ya)
        ya_ref[r0:r0 + CONV_ROWS, :] = ya
        done.append(ya)
    new_hist = ext_scr[tm:tm + hist, :]
    ext_scr[0:hist, :] = new_hist

    pb = p[:, 2 * D_A:]
    sh_scr[bq:bq + tm, :] = pb
    prev = sh_scr[0:tm, :]
    last = pb[tm - bq:, :]
    sh_scr[0:bq, :] = last
    shifto_ref[...] = last
    q = pb + (prev - pb) * mu_ref[...]
    r = q[:, :D_B]
    k = q[:, D_B:2 * D_B]
    v = q[:, 2 * D_B:3 * D_B]
    wa = q[:, 3 * D_B:3 * D_B + LORA_W + LORA_A]
    gl = q[:, 3 * D_B + LORA_W + LORA_A:]
    w = -jax.nn.softplus(-(w0_ref[...] + _dot(jnp.tanh(wa), w2_ref[...]))) - 0.5
    a = jax.nn.sigmoid(a0_ref[...] + _dot(wa, a2_ref[...]))
    kk = k * kkw_ref[...]
    sq_hi, sq_lo = _split(kk * kk)
    ssq = jnp.dot(sq_hi, seg_ref[...], preferred_element_type=F32) + jnp.dot(
        sq_lo, seg_ref[...], preferred_element_type=F32)
    kk = kk * lax.rsqrt(jnp.maximum(ssq, 1e-24))
    def put(ref, val):
        val = val.reshape(tm // bq, bq, D_B)
        pad = ref.shape[1] - tm // bq
        if pad:
            val = jnp.concatenate([val, jnp.zeros((pad, bq, D_B), F32)], axis=0)
        ref[...] = jnp.swapaxes(val, 0, 1)

    g_ref[...] = _dot(jax.nn.sigmoid(gl), g2_ref[...])
    put(r_ref, r)
    put(k_ref, k * (1.0 + (a - 1.0) * kaw_ref[...]))
    put(v_ref, v)
    put(ld_ref, -jnp.exp(w))
    put(kk_ref, kk)
    put(b_ref, kk * a)

    @pl.when(pl.program_id(0) == pl.num_programs(0) - 1)
    def _():
        convo_ref[...] = jnp.swapaxes(ext_scr[0:hist, :].reshape(CONV_A_WIDTH - 1, bq, D_A), 0, 1)


def _ab_in(x, conv0, shift0, w_in, cw, cb, lng, lnb, mu, w0, w2p, a0, a2p, g2, kkw, kaw, seg, *, bq, tm, t_pad):
    x_seq = x.ndim == 3
    n = x.shape[0] * x.shape[1] if x_seq else x.shape[0]
    hist = (CONV_A_WIDTH - 1) * bq
    nt = n // tm
    steps = tm // bq
    assert n % tm == 0 and tm % bq == 0 and tm % CONV_ROWS == 0 and (nt == 1 or tm >= hist)
    t_blk = steps if nt > 1 else t_pad
    assert nt * t_blk == t_pad
    seq = jax.ShapeDtypeStruct((bq, t_pad, D_B), F32)
    seq_spec = pl.BlockSpec((bq, t_blk, D_B), lambda i: (0, i, 0))
    ins = [x, w_in, cw, cb, lng, lnb, mu, w0, w2p, a0, a2p, g2, kkw, kaw, seg, conv0, shift0]
    x_spec = pl.BlockSpec((bq, steps, D_MODEL), lambda i: (0, i, 0)) if x_seq else _rows(tm, D_MODEL)
    in_specs = [x_spec] + [_spec(a) for a in ins[1:]]
    extra_specs = [_rows(tm, D_MODEL)] if x_seq else []
    extra_shapes = [jax.ShapeDtypeStruct((n, D_MODEL), F32)] if x_seq else []
    return pl.pallas_call(
        functools.partial(_ab_in_kernel, bq=bq, tm=tm, x_seq=x_seq),
        grid=(nt,),
        in_specs=in_specs,
        out_specs=[_rows(tm, D_A)] + [seq_spec] * 6
        + [_rows(tm, D_B), _full((bq, CONV_A_WIDTH - 1, D_A)), _full((bq, D_BP))] + extra_specs,
        out_shape=[jax.ShapeDtypeStruct((n, D_A), F32)] + [seq] * 6 + [jax.ShapeDtypeStruct((n, D_B), F32)]
        + [jax.ShapeDtypeStruct((bq, CONV_A_WIDTH - 1, D_A), F32), jax.ShapeDtypeStruct((bq, D_BP), F32)]
        + extra_shapes,
        scratch_shapes=[pltpu.VMEM((hist + tm, D_A), F32), pltpu.VMEM((bq + tm, D_BP), F32)],
        compiler_params=_params(1), name="ab_in")(*map(_arr, ins))


def _head_blocks(y, width):
    lane_head = lax.broadcasted_iota(jnp.int32, y.shape, 1) // width
    return jnp.concatenate([jnp.where(lane_head == h, y, jnp.zeros_like(y)) for h in range(WKV_GROUP)], axis=0)


def _per_head(x, y, width):
    return _dg(x.astype(BF16), _head_blocks(y.astype(BF16), width), 1, 0)


def _per_head_hp(x, y, width):
    xh, xl = _split(x)
    yh, yl = _split(y)
    m = x.shape[0]
    both = _dg(jnp.concatenate([xh, xl], axis=0), _head_blocks(yh, width), 1, 0)
    return both[:m] + (both[m:] + _dg(xh, _head_blocks(yl, width), 1, 0))


def _wkv_kernel(r_ref, k_ref, v_ref, ld_ref, kk_ref, b_ref, s0_ref, lnxg_ref, lnxb_ref, rk_ref, seg_ref,
                y_ref, so_ref, s_scr, *, c, sb, n_sq):
    n_grp = H_B // WKV_GROUP
    gc = WKV_GROUP * c
    blk_r = lax.broadcasted_iota(jnp.int32, (WKV_LANES, WKV_LANES), 0) // HEAD_B
    blk_c = lax.broadcasted_iota(jnp.int32, (WKV_LANES, WKV_LANES), 1) // HEAD_B
    same_head = blk_r == blk_c

    @pl.when(pl.program_id(1) == 0)
    def _():
        for q in range(sb):
            for gi in range(n_grp):
                s = s0_ref[q, gi * WKV_GROUP:(gi + 1) * WKV_GROUP].reshape(WKV_LANES, HEAD_B)
                s_scr[q, gi] = jnp.where(same_head, jnp.concatenate([s] * WKV_GROUP, axis=1), 0.0)

    row = lax.broadcasted_iota(jnp.int32, (c, c), 0)
    col = lax.broadcasted_iota(jnp.int32, (c, c), 1)
    tri = jnp.where(row >= col, 1.0, 0.0).astype(BF16)
    row_g = lax.broadcasted_iota(jnp.int32, (c, gc), 0)
    col_g = lax.broadcasted_iota(jnp.int32, (c, gc), 1) % c
    incl = row_g >= col_g
    strict = row_g > col_g
    eye = jnp.where(row_g == col_g, 1.0, 0.0).astype(F32)

    left, right, v_g, bk_g, p_end, s_prev = [], [], [], [], [], []
    for q in range(sb):
        ld = ld_ref[q]
        ld_hi = ld.astype(BF16)
        ld_rem = ld - ld_hi.astype(F32)
        ld_mid = ld_rem.astype(BF16)
        ld_lo = (ld_rem - ld_mid.astype(F32)).astype(BF16)
        cum = (jnp.dot(tri, ld_hi, preferred_element_type=F32)
               + (jnp.dot(tri, ld_mid, preferred_element_type=F32)
                  + jnp.dot(tri, ld_lo, preferred_element_type=F32)))
        p_in = jnp.exp(cum)
        p_inv = jnp.exp(-cum)
        kk_t = kk_ref[q] * jnp.exp(cum - ld)
        r_t = r_ref[q] * p_in
        b_t = b_ref[q] * p_inv
        k_t = k_ref[q] * p_inv
        v_all = v_ref[q]
        for gi in range(n_grp):
            gl = slice(gi * WKV_LANES, (gi + 1) * WKV_LANES)
            left.append(jnp.concatenate([kk_t[:, gl], r_t[:, gl]], axis=0).astype(BF16))
            b_bf, k_bf = b_t[:, gl].astype(BF16), k_t[:, gl].astype(BF16)
            right.append(jnp.concatenate([_head_blocks(b_bf, HEAD_B), _head_blocks(k_bf, HEAD_B)], axis=0))
            bk_g.append(jnp.concatenate([b_bf, k_bf], axis=0))
            v_g.append(v_all[:, gl])
            p_end.append(p_in[c - 1:c, gl])
            s_prev.append(s_scr[q, gi])

    chains = range(sb * n_grp)
    gram = [_dg(left[i], right[i], 1, 1) for i in chains]
    a_b = [jnp.where(strict, gram[i][:c, :gc], 0.0) for i in chains]
    a_k = [jnp.where(strict, gram[i][:c, gc:], 0.0) for i in chains]
    a_rb = [jnp.where(incl, gram[i][c:, :gc], 0.0) for i in chains]
    a_rk = [jnp.where(incl, gram[i][c:, gc:], 0.0) for i in chains]
    gs = [_dg(left[i], s_prev[i].astype(BF16), 1, 1) for i in chains]
    av = [_per_head(jnp.concatenate([a_k[i], a_rk[i]], axis=0), v_g[i], HEAD_B) for i in chains]
    rhs = [gs[i][:c] + av[i][:c] for i in chains]
    y_v = [gs[i][c:] + av[i][c:] for i in chains]
    n_pow = [-a_b[i] for i in chains]
    t0 = [eye + n_pow[i] for i in chains]
    if n_sq:
        n_pow = [_per_head(n_pow[i], n_pow[i], c) for i in chains]
        for _ in range(n_sq - 1):
            both = [_per_head(jnp.concatenate([t0[i], n_pow[i]], axis=0), n_pow[i], c) for i in chains]
            t0 = [t0[i] + both[i][:c] for i in chains]
            n_pow = [both[i][c:] for i in chains]
        t0 = [t0[i] + _per_head(t0[i], n_pow[i], c) for i in chains]
    res = [eye - t0[i] - _per_head_hp(a_b[i], t0[i], c) for i in chains]
    z = [rhs[i] + _per_head(res[i], rhs[i], HEAD_B) for i in chains]
    u = [-_per_head_hp(t0[i], z[i], HEAD_B) for i in chains]
    y = [y_v[i] + _per_head(a_rb[i], u[i], HEAD_B) for i in chains]
    upd = [_dg(jnp.concatenate([u[i], v_g[i]], axis=0).astype(BF16), bk_g[i], 0, 0) for i in chains]
    s_new = [(s_prev[i] + jnp.where(same_head, upd[i], 0.0)) * p_end[i] for i in chains]
    for i in chains:
        s_scr[i // n_grp, i % n_grp] = s_new[i]

    rows = sb * c

    def head_sum(x):
        hi, lo = _split(x)
        parts = [hi[:, :WKV_LANES], lo[:, :WKV_LANES], hi[:, WKV_LANES:], lo[:, WKV_LANES:]]
        s = jnp.dot(jnp.concatenate(parts, axis=0), seg_ref[...], preferred_element_type=F32)
        return jnp.concatenate([s[:rows] + s[rows:2 * rows], s[2 * rows:3 * rows] + s[3 * rows:]], axis=1)

    flat = lambda ref: ref[...].reshape(rows, D_B)
    y_all = jnp.concatenate([jnp.concatenate(y[q * n_grp:(q + 1) * n_grp], axis=1) for q in range(sb)], axis=0)
    yc = y_all - head_sum(y_all) * (1.0 / HEAD_B)
    yn = yc * lax.rsqrt(head_sum(yc * yc) * (1.0 / HEAD_B) + GN_EPS)
    bonus = head_sum(flat(r_ref) * flat(k_ref) * rk_ref[...]) * flat(v_ref)
    y_ref[...] = (yn * lnxg_ref[...] + lnxb_ref[...] + bonus).reshape(sb, c, D_B)

    @pl.when(pl.program_id(1) == pl.num_programs(1) - 1)
    def _():
        for q in range(sb):
            for gi in range(n_grp):
                for h in range(WKV_GROUP):
                    hl = slice(h * HEAD_B, (h + 1) * HEAD_B)
                    so_ref[q, gi * WKV_GROUP + h] = s_scr[q, gi, hl, hl]


def _wkv(r, k, v, ld, kk, b, s0, lnxg, lnxb, rk, seg, *, c, sb):
    nb, t, _ = r.shape
    assert t % c == 0 and nb % sb == 0
    n_sq = max(int(math.ceil(math.log2(c))) - 1, 0)
    seq = pl.BlockSpec((sb, c, D_B), lambda i, j: (i, j, 0))
    st = pl.BlockSpec((sb, H_B, HEAD_B, HEAD_B), lambda i, j: (i, 0, 0, 0))
    return pl.pallas_call(
        functools.partial(_wkv_kernel, c=c, sb=sb, n_sq=n_sq),
        grid=(nb // sb, t // c),
        in_specs=[seq] * 6 + [st, _spec(lnxg), _spec(lnxb), _spec(rk), _spec(seg)],
        out_specs=[seq, st],
        out_shape=[jax.ShapeDtypeStruct((nb, t, D_B), F32),
                   jax.ShapeDtypeStruct((nb, H_B, HEAD_B, HEAD_B), F32)],
        scratch_shapes=[pltpu.VMEM((sb, H_B // WKV_GROUP, WKV_LANES, WKV_LANES), F32)],
        compiler_params=_params(2), name="wkv")(r, k, v, ld, kk, b, s0, _arr(lnxg), _arr(lnxb), _arr(rk), seg)


def _ab_out_kernel(x_ref, ya_ref, yb_ref, gate_ref, w_ref, g_ref, b_ref, o_ref, *, bq, tm):
    yb = jnp.swapaxes(yb_ref[...], 0, 1)[:tm // bq].reshape(tm, D_B)
    out = _dot(ya_ref[...], w_ref[0:D_A, :]) + _dot(yb * gate_ref[...], w_ref[D_A:, :])
    o_ref[...] = _layer_norm(ALPHA * x_ref[...] + out, g_ref[...], b_ref[...], LN_EPS)


def _ab_out(x, ya, yb, gate, w_out, g, b, *, bq, tm):
    n = x.shape[0]
    nt = n // tm
    t_blk = tm // bq if nt > 1 else yb.shape[1]
    assert n % tm == 0 and nt * t_blk == yb.shape[1]
    return pl.pallas_call(
        functools.partial(_ab_out_kernel, bq=bq, tm=tm), grid=(nt,),
        in_specs=[_rows(tm, D_MODEL), _rows(tm, D_A), pl.BlockSpec((bq, t_blk, D_B), lambda i: (0, i, 0)),
                  _rows(tm, D_B), _spec(w_out), _spec(g), _spec(b)],
        out_specs=_rows(tm, D_MODEL), out_shape=jax.ShapeDtypeStruct((n, D_MODEL), F32),
        compiler_params=_params(1), name="ab_out")(x, ya, yb, gate, _arr(w_out), _arr(g), _arr(b))


def _s5_coef_kernel(lr_ref, li_ref, ldt_ref, br_ref, bi_ref, are_ref, aim_ref, bbr_ref, bbi_ref):
    lr, li = lr_ref[...], li_ref[...]
    dt = jnp.exp(ldt_ref[...])
    mag = jnp.exp(lr * dt)
    ab_re, ab_im = mag * jnp.cos(li * dt), mag * jnp.sin(li * dt)
    den = lr * lr + li * li
    nr, ni = ab_re - 1.0, ab_im
    f_re, f_im = (nr * lr + ni * li) / den, (ni * lr - nr * li) / den
    are_ref[...] = ab_re
    aim_ref[...] = ab_im
    for ch in range(C_GROUP):
        bbr_ref[ch] = f_re * br_ref[ch] - f_im * bi_ref[ch]
        bbi_ref[ch] = f_re * bi_ref[ch] + f_im * br_ref[ch]


def _s5_coef(lam_re, lam_im, log_dt, b_re, b_im):
    br = jnp.transpose(b_re, (2, 0, 1))
    bi = jnp.transpose(b_im, (2, 0, 1))
    gp = jax.ShapeDtypeStruct((G_C, P_C), F32)
    cgp = jax.ShapeDtypeStruct((C_GROUP, G_C, P_C), F32)
    return pl.pallas_call(_s5_coef_kernel, out_shape=[gp, gp, cgp, cgp], name="s5_coef")(
        lam_re, lam_im, log_dt[:, None], br, bi)


def _gelu_tanh(x):
    return 0.5 * x * (1.0 + jnp.tanh(math.sqrt(2.0 / math.pi) * (x + 0.044715 * (x * x * x))))


def _s5_kernel(x_ref, win_ref, bbd_ref, cre_ref, cim_ref, are_ref, aim_ref, dsk_ref, wout_ref, lng_ref,
               lnb_ref, hre0_ref, him0_ref, o_ref, hreo_ref, himo_ref, bre_scr, bim_scr, hre_scr, him_scr,
               *, bq, tm, lb):
    @pl.when(pl.program_id(0) == 0)
    def _():
        hre_scr[...] = hre0_ref[...]
        him_scr[...] = him0_ref[...]

    x = x_ref[...]
    u = _dot(x, win_ref[...])
    ub = u.astype(BF16)
    ys = []
    for j in range(N_SLAB):
        slab = slice(j * SSM_SLAB, (j + 1) * SSM_SLAB)
        bu = jnp.dot(ub[:, j * SLAB:(j + 1) * SLAB], bbd_ref[j], preferred_element_type=F32)
        bre_scr[:, slab] = bu[:, :SSM_SLAB]
        bim_scr[:, slab] = bu[:, SSM_SLAB:]
        for blk in range(SSM_SLAB // lb):
            cs = slice(j * SSM_SLAB + blk * lb, j * SSM_SLAB + (blk + 1) * lb)
            a_re, a_im = are_ref[:, cs], aim_ref[:, cs]
            h_re, h_im = hre_scr[:, cs], him_scr[:, cs]
            for t in range(tm // bq):
                rows = slice(t * bq, (t + 1) * bq)
                h_re, h_im = (a_re * h_re - a_im * h_im + bre_scr[rows, cs],
                              a_re * h_im + a_im * h_re + bim_scr[rows, cs])
                bre_scr[rows, cs] = h_re
                bim_scr[rows, cs] = h_im
            hre_scr[:, cs] = h_re
            him_scr[:, cs] = h_im
        ys.append(_dot(bre_scr[:, slab], cre_ref[j]) - _dot(bim_scr[:, slab], cim_ref[j]))
    hreo_ref[...] = hre_scr[...]
    himo_ref[...] = him_scr[...]
    y = jnp.concatenate(ys, axis=1) + dsk_ref[...] * u
    o = _dot(_gelu_tanh(y), wout_ref[...])
    out = o[:, :D_MODEL] * jax.nn.sigmoid(o[:, D_MODEL:])
    o_ref[...] = _layer_norm(ALPHA * x + out, lng_ref[...], lnb_ref[...], LN_EPS)


def _s5(x, h_re0, h_im0, w_in, bbd, cre, cim, a_re, a_im, dsk, w_out, lng, lnb, *, bq, tm, lb):
    n = x.shape[0]
    assert n % tm == 0 and tm % bq == 0
    ins = [x, w_in, bbd, cre, cim, a_re, a_im, dsk, w_out, lng, lnb, h_re0, h_im0]
    st = jax.ShapeDtypeStruct((bq, D_SSM), F32)
    return pl.pallas_call(
        functools.partial(_s5_kernel, bq=bq, tm=tm, lb=lb),
        grid=(n // tm,),
        in_specs=[_rows(tm, D_MODEL)] + [_spec(a) for a in ins[1:]],
        out_specs=[_rows(tm, D_MODEL), _full((bq, D_SSM)), _full((bq, D_SSM))],
        out_shape=[jax.ShapeDtypeStruct((n, D_MODEL), F32), st, st],
        scratch_shapes=[pltpu.VMEM((tm, D_SSM), F32), pltpu.VMEM((tm, D_SSM), F32),
                        pltpu.VMEM((bq, D_SSM), F32), pltpu.VMEM((bq, D_SSM), F32)],
        compiler_params=_params(1), name="s5")(*map(_arr, ins))


def _ffn_kernel(x_ref, wup_ref, cw_ref, cb_ref, wdn_ref, lng_ref, lnb_ref, st0_ref, o_ref, sto_ref,
                carry_scr, *, bq, tm, seq_out):
    @pl.when(pl.program_id(0) == 0)
    def _():
        carry_scr[...] = st0_ref[...]

    x = x_ref[...]
    xb = x.astype(BF16)
    acc = jnp.zeros((tm, D_MODEL), F32)
    for ci in range(D_FF // FFN_CHUNK):
        cs = slice(ci * FFN_CHUNK, (ci + 1) * FFN_CHUNK)
        gs = slice(D_FF + ci * FFN_CHUNK, D_FF + (ci + 1) * FFN_CHUNK)
        h = jnp.dot(xb, wup_ref[:, cs], preferred_element_type=F32)
        gate = jnp.dot(xb, wup_ref[:, gs], preferred_element_type=F32)
        old = carry_scr[:, cs]
        h_m1 = jnp.concatenate([old[bq:], h[:tm - bq]], axis=0)
        h_m2 = jnp.concatenate([old, h[:tm - 2 * bq]], axis=0)
        cv = cw_ref[0:1, cs] * h_m2 + cw_ref[1:2, cs] * h_m1 + cw_ref[2:3, cs] * h + cb_ref[:, cs]
        act = cv * jax.nn.sigmoid(cv) * gate
        acc = acc + jnp.dot(act.astype(BF16), wdn_ref[cs, :], preferred_element_type=F32)
        carry_scr[:, cs] = h[tm - 2 * bq:]
    out = _layer_norm(ALPHA * x + acc, lng_ref[...], lnb_ref[...], LN_EPS)
    if seq_out:
        o_ref[...] = jnp.swapaxes(out.reshape(tm // bq, bq, D_MODEL), 0, 1)
    else:
        o_ref[...] = out
    sto_ref[...] = carry_scr[...]


def _ffn(x, st0, w_up, cw, cb, w_dn, lng, lnb, *, bq, tm, seq_out):
    n = x.shape[0]
    assert n % tm == 0 and tm >= 2 * bq and D_FF % FFN_CHUNK == 0
    ins = [x, w_up, cw, cb, w_dn, lng, lnb, st0]
    st = jax.ShapeDtypeStruct(st0.shape, F32)
    if seq_out:
        o_shape = jax.ShapeDtypeStruct((bq, n // bq, D_MODEL), F32)
        o_spec = pl.BlockSpec((bq, tm // bq, D_MODEL), lambda i: (0, i, 0))
    else:
        o_shape, o_spec = jax.ShapeDtypeStruct((n, D_MODEL), F32), _rows(tm, D_MODEL)
    return pl.pallas_call(
        functools.partial(_ffn_kernel, bq=bq, tm=tm, seq_out=seq_out),
        grid=(n // tm,),
        in_specs=[_rows(tm, D_MODEL)] + [_spec(a) for a in ins[1:]],
        out_specs=[o_spec, _full(st0.shape)],
        out_shape=[o_shape, st],
        scratch_shapes=[pltpu.VMEM(st0.shape, F32)],
        compiler_params=_params(1), name="ffn")(*map(_arr, ins))


def _trunk(x, st_conv, st_shift, st_wkv, st_re, st_im, st_ffn, pr, *, bq, t_len, tiles):
    hist = CONV_A_WIDTH - 1
    c = min(WKV_CHUNK, 8 * ((t_len + 7) // 8))
    t_pad = c * ((t_len + c - 1) // c)
    o_conv, o_shift, o_wkv, o_re, o_im, o_ffn = [], [], [], [], [], []
    at = lambda name, layer: (pr[name], layer)

    for i in range(DEPTH):
        j = i // 2
        if i % 2 == 0:
            outs = _ab_in(x, (st_conv, j, 1), st_shift[:, j], *(at(n, j) for n in AB_IN_PARAMS), pr["seg"],
                          bq=bq, tm=tiles["ab_in"], t_pad=t_pad)
            ya, r, k, v, ld, kk, bv, gate, conv_n, shift_n = outs[:10]
            if i == 0:
                x = outs[10]
            yb, wkv_n = _wkv(r, k, v, ld, kk, bv, st_wkv[:, j], at("ln_x_g", j), at("ln_x_b", j), at("r_k_b", j),
                             pr["seg_group"], c=c, sb=tiles["wkv_sb"])
            x = _ab_out(x, ya, yb, gate, at("w_out_ab", j), at("ln1_g", i), at("ln1_b", i), bq=bq,
                        tm=tiles["ab_out"])
            o_conv.append(conv_n)
            o_shift.append(shift_n)
            o_wkv.append(wkv_n)
        else:
            x, h_re, h_im = _s5(x, st_re[:, j].reshape(bq, D_SSM), st_im[:, j].reshape(bq, D_SSM),
                                *(at(n, j) for n in S5_PARAMS), at("ln1_g", i), at("ln1_b", i),
                                bq=bq, tm=tiles["s5"], lb=tiles["s5_lb"])
            o_re.append(h_re.reshape(bq, G_C, P_C))
            o_im.append(h_im.reshape(bq, G_C, P_C))
        ffn0 = jnp.transpose(st_ffn[:, i], (1, 0, 2)).reshape((FFN_CONV_WIDTH - 1) * bq, D_FF)
        x, ffn_n = _ffn(x, ffn0, *(at(n, i) for n in FFN_PARAMS), bq=bq, tm=tiles["ffn"],
                        seq_out=i == DEPTH - 1)
        o_ffn.append(jnp.transpose(ffn_n.reshape(FFN_CONV_WIDTH - 1, bq, D_FF), (1, 0, 2)))
    stack = lambda xs: jnp.stack(xs, 1)
    return x, stack(o_conv), stack(o_shift), stack(o_wkv), stack(o_re), stack(o_im), stack(o_ffn)


def _prepare(w_in_ab, conv_a_w, conv_a_b, ln_a_g, ln_a_b, mu_b, w0_b, w2_b, a0_b, a2_b, g2_b, k_k_b, k_a_b,
             r_k_b, ln_x_g, ln_x_b, w_out_ab, w_in_c, lam_re, lam_im, log_dt, b_re, b_im, c_re, c_im, d_skip,
             w_out_c, w_up, conv_f_w, conv_f_b, w_down, ln1_g, ln1_b, ln2_g, ln2_b):
    n_ab, n_c = w_in_ab.shape[0], w_in_c.shape[0]
    row = lambda a: a.reshape(a.shape[0], 1, -1)
    zeros = jnp.zeros((n_ab, LORA_W, D_B), F32)
    eye = jnp.eye(G_SLAB, dtype=F32)
    coefs = [_s5_coef(lam_re[j], lam_im[j], log_dt[j], b_re[j], b_im[j]) for j in range(n_c)]

    def drive(bb):
        bb = bb.reshape(C_GROUP, N_SLAB, G_SLAB, P_C)
        return jnp.einsum("csgp,gh->sgchp", bb, eye).reshape(N_SLAB, SLAB, SSM_SLAB)

    def readout(cc):
        cc = cc.reshape(N_SLAB, G_SLAB, C_GROUP, P_C)
        return jnp.einsum("sgcp,gh->shpgc", cc, eye).reshape(N_SLAB, SSM_SLAB, SLAB)

    return {
        "w_in_ab": w_in_ab.astype(BF16), "conv_a_b": row(conv_a_b),
        "conv_a_w": jnp.broadcast_to(conv_a_w[:, :, None, :], (n_ab, CONV_A_WIDTH, SUBLANES, D_A)),
        "ln_a_g": row(ln_a_g), "ln_a_b": row(ln_a_b), "mu_b": row(mu_b), "w0_b": row(w0_b),
        "w2_pad": jnp.concatenate([w2_b, zeros], axis=1).astype(BF16), "a0_b": row(a0_b),
        "a2_pad": jnp.concatenate([zeros, a2_b], axis=1).astype(BF16), "g2_b": g2_b.astype(BF16),
        "k_k_b": row(k_k_b), "k_a_b": row(k_a_b), "r_k_b": row(r_k_b), "ln_x_g": row(ln_x_g),
        "ln_x_b": row(ln_x_b), "w_out_ab": w_out_ab.astype(BF16),
        "seg": jnp.kron(jnp.eye(H_B, dtype=F32), jnp.ones((HEAD_B, HEAD_B), F32)).astype(BF16),
        "seg_group": jnp.kron(jnp.eye(WKV_GROUP, dtype=F32), jnp.ones((HEAD_B, HEAD_B), F32)).astype(BF16),
        "w_in_c": w_in_c.astype(BF16),
        "bbd": jnp.stack([jnp.concatenate([drive(cf[2]), drive(cf[3])], axis=-1) for cf in coefs]).astype(BF16),
        "cre": jnp.stack([readout(c_re[j]) for j in range(n_c)]).astype(BF16),
        "cim": jnp.stack([readout(c_im[j]) for j in range(n_c)]).astype(BF16),
        "a_re": jnp.stack([cf[0].reshape(1, D_SSM) for cf in coefs]),
        "a_im": jnp.stack([cf[1].reshape(1, D_SSM) for cf in coefs]),
        "d_skip": row(d_skip), "w_out_c": w_out_c.astype(BF16),
        "w_up": w_up.astype(BF16), "conv_f_w": conv_f_w, "conv_f_b": row(conv_f_b),
        "w_down": w_down.astype(BF16),
        "ln1_g": row(ln1_g), "ln1_b": row(ln1_b), "ln2_g": row(ln2_g), "ln2_b": row(ln2_b),
    }


AB_IN_PARAMS = ("w_in_ab", "conv_a_w", "conv_a_b", "ln_a_g", "ln_a_b", "mu_b", "w0_b", "w2_pad", "a0_b", "a2_pad",
                "g2_b", "k_k_b", "k_a_b")
S5_PARAMS = ("w_in_c", "bbd", "cre", "cim", "a_re", "a_im", "d_skip", "w_out_c")
FFN_PARAMS = ("w_up", "conv_f_w", "conv_f_b", "w_down", "ln2_g", "ln2_b")
PROMPT_TILES = {"ab_in": 256, "wkv_sb": 8, "ab_out": 512, "s5": 512, "s5_lb": 512, "ffn": 512}
SAMPLE_TILES = {"ab_in": 512, "wkv_sb": 8, "ab_out": 512, "s5": 512, "s5_lb": 128, "ffn": 512}


def _run_group(x, states, pr, tiles):
    bq, t_len, _ = x.shape
    return _trunk(x, *states, pr, bq=bq, t_len=t_len, tiles=tiles)


def kernel(x_prompt, x_sample, state_conv_a, state_shift_b, state_wkv_b, state_ssm_re, state_ssm_im,
           state_conv_ffn, w_in_ab, conv_a_w, conv_a_b, ln_a_g, ln_a_b, mu_b, w0_b, w2_b, a0_b, a2_b, g2_b,
           k_k_b, k_a_b, r_k_b, ln_x_g, ln_x_b, w_out_ab, w_in_c, lam_re, lam_im, log_dt, b_re, b_im, c_re,
           c_im, d_skip, w_out_c, w_up, conv_f_w, conv_f_b, w_down, ln1_g, ln1_b, ln2_g, ln2_b):
    pr = _prepare(w_in_ab, conv_a_w, conv_a_b, ln_a_g, ln_a_b, mu_b, w0_b, w2_b, a0_b, a2_b, g2_b, k_k_b,
                  k_a_b, r_k_b.reshape(r_k_b.shape[0], D_B), ln_x_g, ln_x_b, w_out_ab, w_in_c, lam_re, lam_im,
                  log_dt, b_re, b_im, c_re, c_im, d_skip, w_out_c, w_up, conv_f_w, conv_f_b, w_down, ln1_g,
                  ln1_b, ln2_g, ln2_b)
    sample_states = (state_conv_a, state_shift_b, state_wkv_b, state_ssm_re, state_ssm_im, state_conv_ffn)
    bp = x_prompt.shape[0]
    prompt_states = tuple(jnp.zeros((bp,) + s.shape[1:], s.dtype) for s in sample_states)
    p = _run_group(x_prompt, prompt_states, pr, PROMPT_TILES)
    s = _run_group(x_sample, sample_states, pr, SAMPLE_TILES)
    return (p[0], s[0], p[1], s[1], p[2], s[2], p[3], s[3], p[4], s[4], p[5], s[5], p[6], s[6])
```
